```python
import jax
import jax.numpy as jnp
from jax import lax
import numpy as np

D_MODEL = 4096
BATCH = 2
SEQ = 8192
DEPTH = 2

GRID_W = 64
CTX_LEN = 256
N_MIXERS = 2
N_MOD = 6
NORM_EPS = 1e-6
FOURIER_GROUPS = 8
DA_HEAD_DIM = 128
DA_V_DIM = 2 * DA_HEAD_DIM
DA_HEADS = D_MODEL // DA_V_DIM
ROPE_BASE = 10000.0
Q_BLOCK = 128
N_EXPERTS = 64
TOP_K = 8
N_GROUPS = 8
TOPK_GROUPS = 4
EXPERT_FF = D_MODEL // 16
SHARED_FF = EXPERT_FF
ROUTED_SCALE = 2.5

kernel_name = 'hybrid_fourier_diffattn_moe_dit'


def rmsnorm(x, g):
    xf = x.astype(jnp.float32)
    y = xf * lax.rsqrt(jnp.mean(xf * xf, axis=-1, keepdims=True) + NORM_EPS)
    return (y * g.astype(jnp.float32)).astype(x.dtype)


def adaln_params(cond, w, b):
    m = jax.nn.silu(cond) @ w + b
    return m.reshape(cond.shape[0], 1, N_MOD, -1)


def modulate(h, shift, scale):
    return h * (1 + scale) + shift


def axial_rope_tables(rows):
    n_freq = DA_HEAD_DIM // 4
    freqs = ROPE_BASE ** (-jnp.arange(n_freq, dtype=jnp.float32) / n_freq)
    r = jnp.broadcast_to(jnp.arange(rows, dtype=jnp.float32)[:, None], (rows, GRID_W)).reshape(-1)
    col = jnp.broadcast_to(jnp.arange(GRID_W, dtype=jnp.float32)[None, :], (rows, GRID_W)).reshape(-1)
    ang_r = r[:, None] * freqs[None, :]
    ang_c = col[:, None] * freqs[None, :]
    return (jnp.cos(ang_r), jnp.sin(ang_r), jnp.cos(ang_c), jnp.sin(ang_c))


def rotate_half(x, cos, sin):
    x1, x2 = jnp.split(x, 2, axis=-1)
    return jnp.concatenate([x1 * cos - x2 * sin, x2 * cos + x1 * sin], axis=-1)


def apply_axial_rope(x, tables):
    cr, sr, cc, sc = [t[None, :, None, None, :].astype(x.dtype) for t in tables]
    x_row, x_col = jnp.split(x, 2, axis=-1)
    return jnp.concatenate([rotate_half(x_row, cr, sr), rotate_half(x_col, cc, sc)], axis=-1)


def fourier_mix(h, wo, bo):
    b, t, d = h.shape
    hg = h.astype(jnp.float32).reshape(b, t, FOURIER_GROUPS, d // FOURIER_GROUPS)
    f = jnp.fft.fftn(hg, axes=(1, 3), norm='ortho').real
    return f.reshape(b, t, d).astype(h.dtype) @ wo + bo


def qkv_split(h, wqkv):
    b, t, _ = h.shape
    q, k, v = jnp.split(h @ wqkv, 3, axis=-1)
    return (q.reshape(b, t, DA_HEADS, 2, DA_HEAD_DIM),
            k.reshape(b, t, DA_HEADS, 2, DA_HEAD_DIM),
            v.reshape(b, t, DA_HEADS, DA_V_DIM))


def diff_attend(q, k, v, lam):
    s = jnp.einsum('bqhcd,bkhcd->bchqk', q, k) * (DA_HEAD_DIM ** -0.5)
    p = jax.nn.softmax(s.astype(jnp.float32), axis=-1)
    a = p[:, 0] - lam * p[:, 1]
    return jnp.einsum('bhqk,bkhv->bqhv', a.astype(v.dtype), v)


def diff_attention(hl, hc, wqkv, wo, lam_p, subln_g, lam_init, tables, need_ctx_out):
    b, s, _ = hl.shape
    lp = lam_p.astype(jnp.float32)
    lam = jnp.exp(jnp.sum(lp[0] * lp[1])) - jnp.exp(jnp.sum(lp[2] * lp[3])) + lam_init
    ql, kl, vl = qkv_split(hl, wqkv)
    qc, kc, vc = qkv_split(hc, wqkv)
    ql = apply_axial_rope(ql, tables)
    kl = apply_axial_rope(kl, tables)
    k_all = jnp.concatenate([kl, kc], axis=1)
    v_all = jnp.concatenate([vl, vc], axis=1)
    nb = s // Q_BLOCK
    q_blocks = jnp.moveaxis(ql.reshape(b, nb, Q_BLOCK, DA_HEADS, 2, DA_HEAD_DIM), 1, 0)
    o_blocks = lax.map(lambda qb: diff_attend(qb, k_all, v_all, lam), q_blocks)
    ol = jnp.moveaxis(o_blocks, 0, 1).reshape(b, s, DA_HEADS, DA_V_DIM)

    def finish(o):
        o = rmsnorm(o, subln_g) * (1.0 - lam_init)
        return o.reshape(o.shape[0], o.shape[1], -1) @ wo

    oc = finish(diff_attend(qc, kc, vc, lam)) if need_ctx_out else None
    return finish(ol), oc


def swiglu(h, w1, w3, w2):
    return (jax.nn.silu(h @ w1) * (h @ w3)) @ w2


def moe(h, router_w, router_bias, w1, w3, w2, sw1, sw3, sw2):
    n = h.shape[0]
    per_group = N_EXPERTS // N_GROUPS
    scores = jax.nn.sigmoid((h @ router_w).astype(jnp.float32))
    sel = scores + router_bias.astype(jnp.float32)
    grp_score = jnp.sum(lax.top_k(sel.reshape(n, N_GROUPS, per_group), 2)[0], axis=-1)
    _, gidx = lax.top_k(grp_score, TOPK_GROUPS)
    gmask = jnp.sum(jax.nn.one_hot(gidx, N_GROUPS, dtype=jnp.float32), axis=1)
    emask = jnp.repeat(gmask, per_group, axis=1)
    sel = jnp.where(emask > 0, sel, -jnp.inf)
    _, eidx = lax.top_k(sel, TOP_K)
    w = jnp.take_along_axis(scores, eidx, axis=1)
    w = w / jnp.sum(w, axis=-1, keepdims=True) * ROUTED_SCALE
    gates = jnp.sum(jax.nn.one_hot(eidx, N_EXPERTS, dtype=jnp.float32) * w[..., None], axis=1).astype(h.dtype)
    out = swiglu(h, sw1, sw3, sw2)
    for e in range(N_EXPERTS):
        out = out + gates[:, e:e + 1] * swiglu(h, w1[e], w3[e], w2[e])
    return out


def setup_inputs(seed: int = 0) -> dict:
    key = jax.random.key(seed)
    ks = jax.random.split(key, 24)
    D = D_MODEL
    n_four = (DEPTH + N_MIXERS - 1) // N_MIXERS
    n_da = DEPTH // N_MIXERS

    def nrm(k, shape, scale):
        return jax.random.normal(k, shape, jnp.float32) * scale

    return {
        'x': nrm(ks[0], (BATCH, SEQ, D), 1.0),
        'c': nrm(ks[1], (BATCH, D), 1.0),
        'ctx': nrm(ks[2], (BATCH, CTX_LEN, D), 1.0),
        'c_ctx': nrm(ks[3], (D,), 1.0),
        'ada_w': nrm(ks[4], (DEPTH, D, N_MOD * D), 0.5 * D ** -0.5),
        'ada_b': nrm(ks[5], (DEPTH, N_MOD * D), 0.02),
        'norm_g': 1.0 + nrm(ks[6], (DEPTH, 2, D), 0.02),
        'fourier_wo': nrm(ks[7], (n_four, D, D), D ** -0.5),
        'fourier_bo': nrm(ks[8], (n_four, D), 0.02),
        'da_wqkv': nrm(ks[9], (n_da, D, 3 * D), D ** -0.5),
        'da_wo': nrm(ks[10], (n_da, D, D), D ** -0.5),
        'da_lambda': nrm(ks[11], (n_da, 4, DA_HEAD_DIM), 0.1),
        'da_subln_g': 1.0 + nrm(ks[12], (n_da, DA_V_DIM), 0.02),
        'router_w': nrm(ks[13], (DEPTH, D, N_EXPERTS), D ** -0.5),
        'router_bias': nrm(ks[14], (DEPTH, N_EXPERTS), 0.01),
        'exp_w1': nrm(ks[15], (DEPTH, N_EXPERTS, D, EXPERT_FF), D ** -0.5),
        'exp_w3': nrm(ks[16], (DEPTH, N_EXPERTS, D, EXPERT_FF), D ** -0.5),
        'exp_w2': nrm(ks[17], (DEPTH, N_EXPERTS, EXPERT_FF, D), EXPERT_FF ** -0.5),
        'shared_w1': nrm(ks[18], (DEPTH, D, SHARED_FF), D ** -0.5),
        'shared_w3': nrm(ks[19], (DEPTH, D, SHARED_FF), D ** -0.5),
        'shared_w2': nrm(ks[20], (DEPTH, SHARED_FF, D), SHARED_FF ** -0.5),
        'final_g': 1.0 + nrm(ks[21], (D,), 0.02),
    }


def reference(x, c, ctx, c_ctx, ada_w, ada_b, norm_g, fourier_wo, fourier_bo,
              da_wqkv, da_wo, da_lambda, da_subln_g, router_w, router_bias,
              exp_w1, exp_w3, exp_w2, shared_w1, shared_w3, shared_w2, final_g):
    b, s, d = x.shape
    rows = s // GRID_W
    tables = axial_rope_tables(rows)
    xl, xc = x, ctx
    for i in range(DEPTH):
        j = i // N_MIXERS
        is_fourier = (i % N_MIXERS) == 0
        need_ctx_out = i < DEPTH - 1
        ml = adaln_params(c, ada_w[i], ada_b[i])
        mc = adaln_params(c_ctx[None, :], ada_w[i], ada_b[i])
        hl = modulate(rmsnorm(xl, norm_g[i, 0]), ml[:, :, 0], ml[:, :, 1])
        oc = None
        if is_fourier:
            ol = fourier_mix(hl, fourier_wo[j], fourier_bo[j])
            if need_ctx_out:
                hc = modulate(rmsnorm(xc, norm_g[i, 0]), mc[:, :, 0], mc[:, :, 1])
                oc = fourier_mix(hc, fourier_wo[j], fourier_bo[j])
        else:
            hc = modulate(rmsnorm(xc, norm_g[i, 0]), mc[:, :, 0], mc[:, :, 1])
            lam_init = 0.8 - 0.6 * float(np.exp(-0.3 * i))
            ol, oc = diff_attention(hl, hc, da_wqkv[j], da_wo[j], da_lambda[j], da_subln_g[j],
                                    lam_init, tables, need_ctx_out)
        xl = xl + ml[:, :, 2] * ol
        hl2 = modulate(rmsnorm(xl, norm_g[i, 1]), ml[:, :, 3], ml[:, :, 4])
        if need_ctx_out:
            xc = xc + mc[:, :, 2] * oc
            hc2 = modulate(rmsnorm(xc, norm_g[i, 1]), mc[:, :, 3], mc[:, :, 4])
            tokens = jnp.concatenate([hl2.reshape(-1, d), hc2.reshape(-1, d)], axis=0)
            y = moe(tokens, router_w[i], router_bias[i], exp_w1[i], exp_w3[i], exp_w2[i],
                    shared_w1[i], shared_w3[i], shared_w2[i])
            xl = xl + ml[:, :, 5] * y[: b * s].reshape(b, s, d)
            xc = xc + mc[:, :, 5] * y[b * s:].reshape(b, -1, d)
        else:
            y = moe(hl2.reshape(-1, d), router_w[i], router_bias[i], exp_w1[i], exp_w3[i], exp_w2[i],
                    shared_w1[i], shared_w3[i], shared_w2[i])
            xl = xl + ml[:, :, 5] * y.reshape(b, s, d)
    return rmsnorm(xl, final_g)
```

```python
import functools
import math

import numpy as np
import jax
import jax.numpy as jnp
from jax import lax
from jax.experimental import pallas as pl
from jax.experimental.pallas import tpu as pltpu

_GRID_W = 64
_N_MOD = 6
_NORM_EPS = 1e-6
_FOURIER_GROUPS = 8
_HEAD_DIM = 128
_V_DIM = 2 * _HEAD_DIM
_ROPE_BASE = 10000.0
_TOP_K = 8
_N_GROUPS = 8
_TOPK_GROUPS = 4
_ROUTED_SCALE = 2.5

_LANE = 128
_VMEM_LIMIT = 56 * 1024 * 1024

_F32 = jnp.float32
_BF16 = jnp.bfloat16
_U32 = jnp.uint32
_HI_MASK = np.uint32(0xFFFF0000)
_NEG_INF = float("-inf")


def _params(sem, **kw):
    return pltpu.CompilerParams(dimension_semantics=sem, vmem_limit_bytes=_VMEM_LIMIT, **kw)


def _pack_bf16_pair(lo, hi):
    ul = pltpu.bitcast(lo.astype(_BF16).astype(_F32), _U32)
    uh = pltpu.bitcast(hi.astype(_BF16).astype(_F32), _U32)
    return (ul >> 16) | (uh & _HI_MASK)


def _unpack_lo(u):
    return pltpu.bitcast(u << 16, _F32)


def _unpack_hi(u):
    return pltpu.bitcast(u & _HI_MASK, _F32)


def _rms_mod(x, g, shift, scale):
    ms = jnp.mean(x * x, axis=-1, keepdims=True)
    return x * lax.rsqrt(ms + _NORM_EPS) * g * (1.0 + scale) + shift


def _adaln_kernel(c_ref, w_ref, b_ref, o_ref):
    c = c_ref[...]
    s = (c * jax.nn.sigmoid(c)).astype(_BF16)
    o_ref[...] = jnp.dot(s, w_ref[...].astype(_BF16), preferred_element_type=_F32) + b_ref[...]


def _adaln(cond8, ada_w, ada_b):
    depth, d, n6 = ada_w.shape
    tn = min(512, n6)
    return pl.pallas_call(
        _adaln_kernel,
        grid=(depth, n6 // tn),
        in_specs=[pl.BlockSpec((8, d), lambda l, j: (0, 0)),
                  pl.BlockSpec((None, d, tn), lambda l, j: (l, 0, j)),
                  pl.BlockSpec((None, 1, tn), lambda l, j: (l, 0, j))],
        out_specs=pl.BlockSpec((None, 8, tn), lambda l, j: (l, 0, j)),
        out_shape=jax.ShapeDtypeStruct((depth, 8, n6), _F32),
        compiler_params=_params(("arbitrary", "arbitrary")),
    )(cond8, ada_w, ada_b.reshape(depth, 1, n6))


def _normmod_chandft_kernel(x_ref, g_ref, mod_ref, tab_ref, o_ref, *, groups):
    h = _rms_mod(x_ref[...], g_ref[...], mod_ref[0:1, :], mod_ref[1:2, :]).astype(_BF16)
    cg = h.shape[-1] // groups
    for g in range(groups):
        r = jnp.dot(h[:, g * cg:(g + 1) * cg], tab_ref[...], preferred_element_type=_F32)
        o_ref[0, :, g * cg:(g + 1) * cg] = r[:, :cg].astype(_BF16)
        o_ref[1, :, g * cg:(g + 1) * cg] = r[:, cg:].astype(_BF16)


def _normmod_chandft(x3, g, mod, mod_per_batch, tab, tm):
    b, t, d = x3.shape
    mod_map = (lambda bi, i: (bi, 0, 0)) if mod_per_batch else (lambda bi, i: (0, 0, 0))
    return pl.pallas_call(
        functools.partial(_normmod_chandft_kernel, groups=_FOURIER_GROUPS),
        grid=(b, t // tm),
        in_specs=[pl.BlockSpec((None, tm, d), lambda bi, i: (bi, i, 0)),
                  pl.BlockSpec((1, d), lambda bi, i: (0, 0)),
                  pl.BlockSpec((None, _N_MOD, d), mod_map),
                  pl.BlockSpec(tab.shape, lambda bi, i: (0, 0))],
        out_specs=pl.BlockSpec((None, 2, tm, d), lambda bi, i: (bi, 0, i, 0)),
        out_shape=jax.ShapeDtypeStruct((b, 2, t, d), _BF16),
        compiler_params=_params(("arbitrary", "arbitrary")),
    )(x3, g, mod, tab)


def _normmod_kernel(x_ref, g_ref, mod_ref, o_ref):
    o_ref[...] = _rms_mod(x_ref[...], g_ref[...], mod_ref[0:1, :], mod_ref[1:2, :]).astype(_BF16)


def _normmod(x2, row0, rows, g, mod, rows_per_mod, mod0, tm):
    d = x2.shape[1]
    t0 = row0 // tm
    tpm = rows_per_mod // tm
    return pl.pallas_call(
        _normmod_kernel,
        grid=(rows // tm,),
        in_specs=[pl.BlockSpec((tm, d), lambda i: (t0 + i, 0)),
                  pl.BlockSpec((1, d), lambda i: (0, 0)),
                  pl.BlockSpec((None, _N_MOD, d), lambda i: (mod0 + i // tpm, 0, 0))],
        out_specs=pl.BlockSpec((tm, d), lambda i: (i, 0)),
        out_shape=jax.ShapeDtypeStruct((rows, d), _BF16),
        compiler_params=_params(("arbitrary",)),
    )(x2, g, mod)


def _dft_step1_kernel(y_ref, m1_ref, tw_ref, o_ref, *, n1):
    d = y_ref.shape[-1]
    r = jnp.dot(m1_ref[...], y_ref[...].reshape(2 * n1, d), preferred_element_type=_F32)
    a_re, a_im = r[:n1], r[n1:]
    tc, ts = tw_ref[0], tw_ref[1]
    o_ref[0] = (a_re * tc + a_im * ts).astype(_BF16)
    o_ref[1] = (a_im * tc - a_re * ts).astype(_BF16)


def _dft_step3_kernel(b_ref, m3_ref, o_ref, *, k1b, n2):
    d = b_ref.shape[-1]
    for j in range(k1b):
        rhs = b_ref[:, j].reshape(2 * n2, d)
        o_ref[:, j * d:(j + 1) * d] = jnp.dot(m3_ref[...], rhs, preferred_element_type=_F32).astype(_BF16)


def _dft_tables(t, n1, n2, cg):
    def cs(n, rows, cols):
        ang = 2.0 * np.pi * ((np.arange(rows, dtype=np.int64)[:, None] * np.arange(cols, dtype=np.int64)[None, :]) % n) / n
        return np.cos(ang), np.sin(ang)

    cc, sc = cs(cg, cg, cg)
    chan = np.concatenate([cc, sc], axis=1) / math.sqrt(cg)
    c1, s1 = cs(n1, n1, n1)
    m1 = np.block([[c1, -s1], [-s1, -c1]])
    ct, st = cs(t, n2, n1)
    tw = np.stack([ct, st], axis=1)[..., None] / math.sqrt(t)
    c2, s2 = cs(n2, n2, n2)
    m3 = np.concatenate([c2, s2], axis=1)
    return (jnp.asarray(chan, _BF16), jnp.asarray(m1, _BF16), jnp.asarray(tw, _F32), jnp.asarray(m3, _BF16))


def _seq_dft(y, m1, tw, m3, n1, n2, k1b):
    b, _, t, d = y.shape
    yv = y.reshape(b, 2, n1, n2 * d)
    bt = pl.pallas_call(
        functools.partial(_dft_step1_kernel, n1=n1),
        grid=(b, n2),
        in_specs=[pl.BlockSpec((None, 2, n1, d), lambda bi, j: (bi, 0, 0, j)),
                  pl.BlockSpec(m1.shape, lambda bi, j: (0, 0)),
                  pl.BlockSpec((None, 2, n1, 1), lambda bi, j: (j, 0, 0, 0))],
        out_specs=pl.BlockSpec((None, 2, n1, d), lambda bi, j: (bi, 0, 0, j)),
        out_shape=jax.ShapeDtypeStruct(yv.shape, _BF16),
        compiler_params=_params(("arbitrary", "arbitrary")),
    )(yv, m1, tw)
    btv = bt.reshape(b, 2, n1, n2, d)
    f = pl.pallas_call(
        functools.partial(_dft_step3_kernel, k1b=k1b, n2=n2),
        grid=(b, n1 // k1b),
        in_specs=[pl.BlockSpec((None, 2, k1b, n2, d), lambda bi, j: (bi, 0, j, 0, 0)),
                  pl.BlockSpec(m3.shape, lambda bi, j: (0, 0))],
        out_specs=pl.BlockSpec((None, n2, k1b * d), lambda bi, j: (bi, 0, j)),
        out_shape=jax.ShapeDtypeStruct((b, n2, n1 * d), _BF16),
        compiler_params=_params(("arbitrary", "arbitrary")),
    )(btv, m3)
    return f.reshape(b, t, d)


def _ctx_dft_kernel(tab_ref, y_ref, o_ref):
    o_ref[...] = jnp.dot(tab_ref[...], y_ref[...], preferred_element_type=_F32).astype(_BF16)


def _ctx_seq_dft(y, tn):
    b, _, t, d = y.shape
    ang = 2.0 * np.pi * ((np.arange(t, dtype=np.int64)[:, None] * np.arange(t, dtype=np.int64)[None, :]) % t) / t
    tab = jnp.asarray(np.concatenate([np.cos(ang), -np.sin(ang)], axis=1) / math.sqrt(t), _BF16)
    return pl.pallas_call(
        _ctx_dft_kernel,
        grid=(b, d // tn),
        in_specs=[pl.BlockSpec(tab.shape, lambda bi, j: (0, 0)),
                  pl.BlockSpec((None, 2 * t, tn), lambda bi, j: (bi, 0, j))],
        out_specs=pl.BlockSpec((None, t, tn), lambda bi, j: (bi, 0, j)),
        out_shape=jax.ShapeDtypeStruct((b, t, d), _BF16),
        compiler_params=_params(("arbitrary", "arbitrary")),
    )(tab, y.reshape(b, 2 * t, d))


def _mm_kernel(*refs, has_bias, has_res):
    a_ref, w_ref = refs[0], refs[1]
    o_ref = refs[-1]
    acc = jnp.dot(a_ref[...], w_ref[...], preferred_element_type=_F32)
    k = 2
    if has_bias:
        acc = acc + refs[k][...]
        k += 1
    if has_res:
        acc = refs[k][...] + refs[k + 1][...] * acc
    o_ref[...] = acc.astype(o_ref.dtype)


def _mm(a, w, *, n_out, w_col0=0, bias=None, res=None, res_row0=0, gate=None, rows_per_gate=None, gate0=0,
        out_dtype=_F32, tm, tn):
    m, k = a.shape
    wj0 = w_col0 // tn
    in_specs = [pl.BlockSpec((tm, k), lambda i, j: (i, 0)),
                pl.BlockSpec((k, tn), lambda i, j: (0, wj0 + j))]
    args = [a, w]
    if bias is not None:
        in_specs.append(pl.BlockSpec((1, tn), lambda i, j: (0, j)))
        args.append(bias)
    if res is not None:
        r0 = res_row0 // tm
        tpg = rows_per_gate // tm
        in_specs.append(pl.BlockSpec((tm, tn), lambda i, j: (r0 + i, j)))
        in_specs.append(pl.BlockSpec((None, 1, tn), lambda i, j: (gate0 + i // tpg, 0, j)))
        args += [res, gate]
    return pl.pallas_call(
        functools.partial(_mm_kernel, has_bias=bias is not None, has_res=res is not None),
        grid=(m // tm, n_out // tn),
        in_specs=in_specs,
        out_specs=pl.BlockSpec((tm, tn), lambda i, j: (i, j)),
        out_shape=jax.ShapeDtypeStruct((m, n_out), out_dtype),
        compiler_params=_params(("arbitrary", "arbitrary")),
    )(*args)


def _qkv_kernel(a_ref, w_ref, cos_ref, sin_ref, o_ref, *, nq, nk, scale):
    j = pl.program_id(1)
    acc = jnp.dot(a_ref[...], w_ref[...], preferred_element_type=_F32)
    tn = acc.shape[1]

    @pl.when(j >= nq + nk)
    def _():
        o_ref[...] = acc.astype(_BF16)

    @pl.when(j < nq + nk)
    def _():
        cos = cos_ref[...]
        sin = sin_ref[...]
        lane = lax.broadcasted_iota(jnp.int32, cos.shape, 1)
        first = (lane % (_HEAD_DIM // 2)) < (_HEAD_DIM // 4)
        sc = jnp.where(j < nq, scale, 1.0).astype(_F32)
        for u in range(tn // _HEAD_DIM):
            x = acc[:, u * _HEAD_DIM:(u + 1) * _HEAD_DIM]
            sw = jnp.where(first, pltpu.roll(x, _HEAD_DIM - _HEAD_DIM // 4, 1), pltpu.roll(x, _HEAD_DIM // 4, 1))
            o_ref[:, u * _HEAD_DIM:(u + 1) * _HEAD_DIM] = ((x * cos + sw * sin) * sc).astype(_BF16)


def _rope_tables(s):
    n_freq = _HEAD_DIM // 4
    freqs = _ROPE_BASE ** (-jnp.arange(n_freq, dtype=_F32) / n_freq)
    pos = jnp.arange(s, dtype=jnp.int32)
    ang_r = (pos // _GRID_W).astype(_F32)[:, None] * freqs[None, :]
    ang_c = (pos % _GRID_W).astype(_F32)[:, None] * freqs[None, :]
    cos = jnp.concatenate([jnp.cos(ang_r)] * 2 + [jnp.cos(ang_c)] * 2, axis=1)
    sin = jnp.concatenate([-jnp.sin(ang_r), jnp.sin(ang_r), -jnp.sin(ang_c), jnp.sin(ang_c)], axis=1)
    return cos, sin


def _qkv(a, w, cos, sin, s, tm, tn):
    m, k = a.shape
    d = w.shape[1] // 3
    tps = s // tm
    return pl.pallas_call(
        functools.partial(_qkv_kernel, nq=d // tn, nk=d // tn, scale=_HEAD_DIM ** -0.5),
        grid=(m // tm, 3 * d // tn),
        in_specs=[pl.BlockSpec((tm, k), lambda i, j: (i, 0)),
                  pl.BlockSpec((k, tn), lambda i, j: (0, j)),
                  pl.BlockSpec((tm, _HEAD_DIM), lambda i, j: (i % tps, 0)),
                  pl.BlockSpec((tm, _HEAD_DIM), lambda i, j: (i % tps, 0))],
        out_specs=pl.BlockSpec((tm, tn), lambda i, j: (i, j)),
        out_shape=jax.ShapeDtypeStruct((m, 3 * d), _BF16),
        compiler_params=_params(("arbitrary", "arbitrary")),
    )(a, w, cos, sin)


def _attn_kernel(q_ref, k_ref, v_ref, kc_ref, vc_ref, lam_ref, g_ref, o_ref, m_sc, l_sc, acc_sc, *, tk, n_chunks, lam_init):
    m_sc[...] = jnp.full(m_sc.shape, _NEG_INF, _F32)
    l_sc[...] = jnp.zeros(l_sc.shape, _F32)
    acc_sc[...] = jnp.zeros(acc_sc.shape, _F32)
    q = q_ref[...]

    def step(kb, vb):
        for c in range(2):
            s = lax.dot_general(q[:, c * _HEAD_DIM:(c + 1) * _HEAD_DIM], kb[:, c * _HEAD_DIM:(c + 1) * _HEAD_DIM],
                                (((1,), (1,)), ((), ())), preferred_element_type=_F32)
            m_old = m_sc[c]
            m_new = jnp.maximum(m_old, jnp.max(s, axis=-1, keepdims=True))
            alpha = jnp.exp(m_old - m_new)
            p = jnp.exp(s - m_new)
            l_sc[c] = alpha * l_sc[c] + jnp.sum(p, axis=-1, keepdims=True)
            acc_sc[c] = alpha * acc_sc[c] + jnp.dot(p.astype(_BF16), vb, preferred_element_type=_F32)
            m_sc[c] = m_new

    def body(i, carry):
        off = pl.multiple_of(i * tk, tk)
        step(k_ref[pl.ds(off, tk), :], v_ref[pl.ds(off, tk), :])
        return carry

    lax.fori_loop(0, n_chunks, body, 0)
    step(kc_ref[...], vc_ref[...])

    lp = lam_ref[...]
    lam = (jnp.exp(jnp.sum(lp[0:1] * lp[1:2], axis=-1, keepdims=True))
           - jnp.exp(jnp.sum(lp[2:3] * lp[3:4], axis=-1, keepdims=True)) + lam_init)
    o = acc_sc[0] / l_sc[0] - lam * (acc_sc[1] / l_sc[1])
    ms = jnp.mean(o * o, axis=-1, keepdims=True)
    o_ref[...] = (o * lax.rsqrt(ms + _NORM_EPS) * g_ref[...] * (1.0 - lam_init)).astype(_BF16)


def _attention(qkv, kvc, lam_p, subln_g, lam_init, tq, tk):
    b, s, d3 = qkv.shape
    d = d3 // 3
    h = d // _V_DIM
    c = kvc.shape[1]
    return pl.pallas_call(
        functools.partial(_attn_kernel, tk=tk, n_chunks=s // tk, lam_init=lam_init),
        grid=(b, h, s // tq),
        in_specs=[pl.BlockSpec((None, tq, _V_DIM), lambda bi, hi, qi: (bi, qi, hi)),
                  pl.BlockSpec((None, s, _V_DIM), lambda bi, hi, qi: (bi, 0, h + hi)),
                  pl.BlockSpec((None, s, _V_DIM), lambda bi, hi, qi: (bi, 0, 2 * h + hi)),
                  pl.BlockSpec((None, c, _V_DIM), lambda bi, hi, qi: (bi, 0, hi)),
                  pl.BlockSpec((None, c, _V_DIM), lambda bi, hi, qi: (bi, 0, h + hi)),
                  pl.BlockSpec(lam_p.shape, lambda bi, hi, qi: (0, 0)),
                  pl.BlockSpec((1, _V_DIM), lambda bi, hi, qi: (0, 0))],
        out_specs=pl.BlockSpec((None, tq, _V_DIM), lambda bi, hi, qi: (bi, qi, hi)),
        out_shape=jax.ShapeDtypeStruct((b, s, d), _BF16),
        scratch_shapes=[pltpu.VMEM((2, tq, 1), _F32), pltpu.VMEM((2, tq, 1), _F32), pltpu.VMEM((2, tq, _V_DIM), _F32)],
        compiler_params=_params(("arbitrary", "arbitrary", "arbitrary")),
    )(qkv, qkv, qkv, kvc, kvc, lam_p, subln_g)


def _route_kernel(*refs, na, two_inputs, n_exp):
    if two_inputs:
        xa_ref, xb_ref = refs[0], refs[1]
        refs = refs[2:]
    else:
        xa_ref, xb_ref = refs[0], None
        refs = refs[1:]
    g_ref, mod_ref, wt_ref, bias_ref, hp_ref, eidx_ref, rank_ref, gate_ref, cnt_ref, carry_sc = refs
    i = pl.program_id(0)

    @pl.when(i == 0)
    def _():
        carry_sc[...] = jnp.zeros(carry_sc.shape, _F32)

    x = xa_ref[...]
    if two_inputs:
        x = jnp.where(i < na, x, xb_ref[...])
    tm, d = x.shape
    h = _rms_mod(x, g_ref[...], mod_ref[3:4, :], mod_ref[4:5, :])
    hp_ref[...] = _pack_bf16_pair(h[:, :d // 2], h[:, d // 2:])

    h_hi = h.astype(_BF16)
    h_lo = (h - h_hi.astype(_F32)).astype(_BF16)
    w = wt_ref[...]
    w_hi = w.astype(_BF16)
    w_lo = (w - w_hi.astype(_F32)).astype(_BF16)
    dn = (((1,), (1,)), ((), ()))
    logits = (lax.dot_general(w_hi, h_hi, dn, preferred_element_type=_F32)
              + lax.dot_general(w_lo, h_hi, dn, preferred_element_type=_F32)
              + lax.dot_general(w_hi, h_lo, dn, preferred_element_type=_F32))
    scores = jax.nn.sigmoid(logits)
    sel = scores + bias_ref[...]

    pg = n_exp // _N_GROUPS
    shp = (_N_GROUPS, pg, tm)
    sel3 = sel.reshape(shp)
    scores3 = scores.reshape(shp)
    midx = lax.broadcasted_iota(jnp.int32, shp, 1).astype(_F32)
    gidx = lax.broadcasted_iota(jnp.int32, (_N_GROUPS, 1, tm), 0).astype(_F32)
    eid3 = lax.broadcasted_iota(jnp.int32, shp, 0).astype(_F32) * pg + midx

    m1 = jnp.max(sel3, axis=1, keepdims=True)
    f1 = jnp.min(jnp.where(sel3 == m1, midx, float(pg)), axis=1, keepdims=True)
    m2 = jnp.max(jnp.where(midx == f1, _NEG_INF, sel3), axis=1, keepdims=True)
    grp = m1 + m2

    gsel = jnp.zeros((_N_GROUPS, 1, tm), _F32)
    for _ in range(_TOPK_GROUPS):
        m = jnp.max(grp, axis=0, keepdims=True)
        f = jnp.min(jnp.where(grp == m, gidx, float(_N_GROUPS)), axis=0, keepdims=True)
        hit = gidx == f
        gsel = jnp.where(hit, 1.0, gsel)
        grp = jnp.where(hit, _NEG_INF, grp)
    cur = jnp.where(gsel > 0.0, sel3, _NEG_INF)

    chosen = jnp.zeros(shp, _F32)
    firsts = []
    for _ in range(_TOP_K):
        m = jnp.max(jnp.max(cur, axis=1, keepdims=True), axis=0, keepdims=True)
        f = jnp.min(jnp.min(jnp.where(cur == m, eid3, float(n_exp)), axis=1, keepdims=True), axis=0, keepdims=True)
        hit = eid3 == f
        chosen = jnp.where(hit, 1.0, chosen)
        cur = jnp.where(hit, _NEG_INF, cur)
        firsts.append(f)

    wsel = scores3 * chosen
    wsum = jnp.sum(jnp.sum(wsel, axis=1, keepdims=True), axis=0, keepdims=True)
    gates3 = wsel / wsum * _ROUTED_SCALE

    ch2 = chosen.reshape(n_exp, tm)
    upper = (lax.broadcasted_iota(jnp.int32, (tm, tm), 0) < lax.broadcasted_iota(jnp.int32, (tm, tm), 1))
    prefix = jnp.dot(ch2.astype(_BF16), jnp.where(upper, 1.0, 0.0).astype(_BF16), preferred_element_type=_F32)
    carry = carry_sc[...]
    rank3 = (carry + prefix).reshape(shp)
    new_carry = carry + jnp.sum(ch2, axis=-1, keepdims=True)
    carry_sc[...] = new_carry
    cnt_ref[...] = new_carry

    for r in range(_TOP_K):
        hit = eid3 == firsts[r]
        rk = jnp.sum(jnp.sum(jnp.where(hit, rank3, 0.0), axis=1, keepdims=True), axis=0, keepdims=True)
        gt = jnp.sum(jnp.sum(jnp.where(hit, gates3, 0.0), axis=1, keepdims=True), axis=0, keepdims=True)
        eidx_ref[r:r + 1, :] = firsts[r].reshape(1, tm).astype(jnp.int32)
        rank_ref[r:r + 1, :] = rk.reshape(1, tm).astype(jnp.int32)
        gate_ref[r:r + 1, :] = gt.reshape(1, tm)


def _route(xa, na_rows, xb, g, modt, rows_per_mod, wt, bias, tm):
    d = xa.shape[1]
    n_exp = wt.shape[0]
    na = na_rows // tm
    nb = 0 if xb is None else xb.shape[0] // tm
    n = (na + nb) * tm
    tpm = rows_per_mod // tm
    n_mod = modt.shape[0]
    in_specs = [pl.BlockSpec((tm, d), lambda i: (jnp.minimum(i, na - 1), 0))]
    args = [xa]
    if xb is not None:
        in_specs.append(pl.BlockSpec((tm, d), lambda i: (jnp.maximum(i - na, 0), 0)))
        args.append(xb)
    in_specs += [pl.BlockSpec((1, d), lambda i: (0, 0)),
                 pl.BlockSpec((None, _N_MOD, d), lambda i: (jnp.minimum(i // tpm, n_mod - 1), 0, 0)),
                 pl.BlockSpec((n_exp, d), lambda i: (0, 0)),
                 pl.BlockSpec((n_exp, 1), lambda i: (0, 0))]
    args += [g, modt, wt, bias]
    return pl.pallas_call(
        functools.partial(_route_kernel, na=na, two_inputs=xb is not None, n_exp=n_exp),
        grid=(na + nb,),
        in_specs=in_specs,
        out_specs=[pl.BlockSpec((tm, d // 2), lambda i: (i, 0)),
                   pl.BlockSpec((_TOP_K, tm), lambda i: (0, i)),
                   pl.BlockSpec((_TOP_K, tm), lambda i: (0, i)),
                   pl.BlockSpec((_TOP_K, tm), lambda i: (0, i)),
                   pl.BlockSpec((n_exp, 1), lambda i: (0, 0))],
        out_shape=[jax.ShapeDtypeStruct((n, d // 2), _U32),
                   jax.ShapeDtypeStruct((_TOP_K, n), jnp.int32),
                   jax.ShapeDtypeStruct((_TOP_K, n), jnp.int32),
                   jax.ShapeDtypeStruct((_TOP_K, n), _F32),
                   jax.ShapeDtypeStruct((n_exp, 1), _F32)],
        scratch_shapes=[pltpu.VMEM((n_exp, 1), _F32)],
        compiler_params=_params(("arbitrary",)),
    )(*args)


def _dispatch_kernel(dest_ref, hp_ref, xs_ref, sem):
    tm = hp_ref.shape[0]

    def row_copy(n, k):
        return pltpu.make_async_copy(hp_ref.at[pl.ds(n, 1), :], xs_ref.at[pl.ds(dest_ref[k, n], 1), :], sem)

    def body(n, carry):
        for k in range(_TOP_K):
            row_copy(n, k).start()
        return carry

    lax.fori_loop(0, tm, body, 0)
    for _ in range(_TOP_K):
        pltpu.make_async_copy(hp_ref, xs_ref.at[pl.ds(0, tm), :], sem).wait()


def _dispatch(hp, dest8, tm):
    n, dh = hp.shape
    return pl.pallas_call(
        _dispatch_kernel,
        grid=(n // tm,),
        in_specs=[pl.BlockSpec((_TOP_K, tm), lambda i: (0, i), memory_space=pltpu.SMEM),
                  pl.BlockSpec((tm, dh), lambda i: (i, 0))],
        out_specs=pl.BlockSpec(memory_space=pl.ANY),
        out_shape=jax.ShapeDtypeStruct((n * _TOP_K, dh), _U32),
        scratch_shapes=[pltpu.SemaphoreType.DMA(())],
        compiler_params=_params(("arbitrary",), disable_bounds_checks=True),
    )(dest8, hp)


def _swiglu_packed(xu, w1, w3, w2):
    dh = xu.shape[1]
    x_lo = _unpack_lo(xu).astype(_BF16)
    x_hi = _unpack_hi(xu).astype(_BF16)
    h1 = (jnp.dot(x_lo, w1[:dh], preferred_element_type=_F32) + jnp.dot(x_hi, w1[dh:], preferred_element_type=_F32))
    h3 = (jnp.dot(x_lo, w3[:dh], preferred_element_type=_F32) + jnp.dot(x_hi, w3[dh:], preferred_element_type=_F32))
    a = (h1 * jax.nn.sigmoid(h1) * h3).astype(_BF16)
    return jnp.dot(a, w2, preferred_element_type=_F32)


def _expert_kernel(tile_ref, exp_ref, lo_ref, hi_ref, first_ref, newexp_ref, valid_ref,
                   x_ref, w1_ref, w3_ref, w2_ref, y_ref, w1b, w3b, w2b):
    v = pl.program_id(0)

    @pl.when(newexp_ref[v] == 1)
    def _():
        w1b[...] = w1_ref[...].astype(_BF16)
        w3b[...] = w3_ref[...].astype(_BF16)
        w2b[...] = w2_ref[...].astype(_BF16)

    @pl.when(valid_ref[v] == 1)
    def _():
        xu = x_ref[...]
        tmx, dh = xu.shape
        y = _swiglu_packed(xu, w1b[...], w3b[...], w2b[...])
        packed = _pack_bf16_pair(y[:, :dh], y[:, dh:])
        rows = lax.broadcasted_iota(jnp.int32, (tmx, 1), 0)
        mine = (rows >= lo_ref[v]) & (rows < hi_ref[v])

        @pl.when(first_ref[v] == 1)
        def _():
            y_ref[...] = jnp.where(mine, packed, jnp.zeros_like(packed))

        @pl.when(first_ref[v] == 0)
        def _():
            y_ref[...] = jnp.where(mine, packed, y_ref[...])


def _expert_visits(counts, n_rows, tmx):
    n_exp = counts.shape[0]
    n_tiles = n_rows // tmx
    n_vis = n_tiles + n_exp - 1
    ends = jnp.cumsum(counts)
    starts = ends - counts
    t_first = starts // tmx
    t_last = jnp.where(counts > 0, (ends - 1) // tmx, t_first - 1)
    nv = t_last - t_first + 1
    v_end = jnp.cumsum(nv)
    v_start = v_end - nv
    total = v_end[-1]
    v = jnp.arange(n_vis, dtype=jnp.int32)
    vc = jnp.minimum(v, total - 1)
    e = jnp.searchsorted(v_end, vc, side="right").astype(jnp.int32)
    tile = (t_first[e] + (vc - v_start[e])).astype(jnp.int32)
    lo = jnp.clip(starts[e] - tile * tmx, 0, tmx).astype(jnp.int32)
    hi = jnp.clip(ends[e] - tile * tmx, 0, tmx).astype(jnp.int32)
    valid = (v < total).astype(jnp.int32)
    prev_tile = jnp.concatenate([jnp.full((1,), -1, jnp.int32), tile[:-1]])
    prev_e = jnp.concatenate([jnp.full((1,), -1, jnp.int32), e[:-1]])
    first = ((tile != prev_tile) & (valid == 1)).astype(jnp.int32)
    newexp = ((e != prev_e) & (valid == 1)).astype(jnp.int32)
    return tile, e, lo, hi, first, newexp, valid


def _experts(xs, counts, w1, w3, w2, tmx):
    p, dh = xs.shape
    n_exp, d, f = w1.shape
    meta = _expert_visits(counts, p, tmx)
    n_vis = meta[0].shape[0]
    grid_spec = pltpu.PrefetchScalarGridSpec(
        num_scalar_prefetch=7,
        grid=(n_vis,),
        in_specs=[pl.BlockSpec((tmx, dh), lambda v, t, e, *_: (t[v], 0)),
                  pl.BlockSpec((None, d, f), lambda v, t, e, *_: (e[v], 0, 0)),
                  pl.BlockSpec((None, d, f), lambda v, t, e, *_: (e[v], 0, 0)),
                  pl.BlockSpec((None, f, d), lambda v, t, e, *_: (e[v], 0, 0))],
        out_specs=pl.BlockSpec((tmx, dh), lambda v, t, e, *_: (t[v], 0)),
        scratch_shapes=[pltpu.VMEM((d, f), _BF16), pltpu.VMEM((d, f), _BF16), pltpu.VMEM((f, d), _BF16)],
    )
    return pl.pallas_call(
        _expert_kernel,
        grid_spec=grid_spec,
        out_shape=jax.ShapeDtypeStruct((p, dh), _U32),
        compiler_params=_params(("arbitrary",)),
    )(*meta, xs, w1, w3, w2)


def _combine_kernel(*refs, na, n_steps, two_inputs, final_norm):
    dcur_ref, dnext_ref, w_ref, hp_ref, xa_ref = refs[:5]
    refs = refs[5:]
    xb_ref = None
    if two_inputs:
        xb_ref, refs = refs[0], refs[1:]
    mod_ref, s1_ref, s3_ref, s2_ref = refs[:4]
    refs = refs[4:]
    fg_ref = None
    if final_norm:
        fg_ref, refs = refs[0], refs[1:]
    ys_ref, o_ref, gbuf, sems = refs
    i = pl.program_id(0)
    slot = i % 2
    tm, dh = hp_ref.shape

    def row_copy(dref, sl, n, k):
        return pltpu.make_async_copy(ys_ref.at[pl.ds(dref[k, n], 1), :], gbuf.at[sl, k, pl.ds(n, 1), :], sems.at[sl])

    def issue(dref, sl):
        def body(n, carry):
            for k in range(_TOP_K):
                row_copy(dref, sl, n, k).start()
            return carry

        lax.fori_loop(0, tm, body, 0)

    @pl.when(i == 0)
    def _():
        issue(dcur_ref, 0)

    @pl.when(i + 1 < n_steps)
    def _():
        issue(dnext_ref, 1 - slot)

    y_sh = _swiglu_packed(hp_ref[...], s1_ref[...], s3_ref[...], s2_ref[...])

    for k in range(_TOP_K):
        pltpu.make_async_copy(ys_ref.at[pl.ds(0, tm), :], gbuf.at[slot, k], sems.at[slot]).wait()

    acc_lo = y_sh[:, :dh]
    acc_hi = y_sh[:, dh:]
    for k in range(_TOP_K):
        u = gbuf[slot, k]
        wk = w_ref[:, k:k + 1]
        acc_lo = acc_lo + wk * _unpack_lo(u)
        acc_hi = acc_hi + wk * _unpack_hi(u)

    x = xa_ref[...]
    if two_inputs:
        x = jnp.where(i < na, x, xb_ref[...])
    gate = mod_ref[5:6, :]
    out_lo = x[:, :dh] + gate[:, :dh] * acc_lo
    out_hi = x[:, dh:] + gate[:, dh:] * acc_hi
    if final_norm:
        ms = (jnp.sum(out_lo * out_lo, axis=-1, keepdims=True)
              + jnp.sum(out_hi * out_hi, axis=-1, keepdims=True)) / (2 * dh)
        r = lax.rsqrt(ms + _NORM_EPS)
        fg = fg_ref[...]
        out_lo = out_lo * r * fg[:, :dh]
        out_hi = out_hi * r * fg[:, dh:]
    o_ref[:, :dh] = out_lo
    o_ref[:, dh:] = out_hi


def _combine(ys, dest8, w8t, hp, xa, na_rows, xb, modt, rows_per_mod, s1, s3, s2, final_g, tm):
    n, dh = hp.shape
    d = 2 * dh
    f = s1.shape[1]
    n_steps = n // tm
    na = na_rows // tm
    tpm = rows_per_mod // tm
    n_mod = modt.shape[0]
    smem_spec = lambda fn: pl.BlockSpec((_TOP_K, tm), fn, memory_space=pltpu.SMEM)
    in_specs = [smem_spec(lambda i: (0, i)),
                smem_spec(lambda i: (0, jnp.minimum(i + 1, n_steps - 1))),
                pl.BlockSpec((tm, _TOP_K), lambda i: (i, 0)),
                pl.BlockSpec((tm, dh), lambda i: (i, 0)),
                pl.BlockSpec((tm, d), lambda i: (jnp.minimum(i, na - 1), 0))]
    args = [dest8, dest8, w8t, hp, xa]
    if xb is not None:
        in_specs.append(pl.BlockSpec((tm, d), lambda i: (jnp.maximum(i - na, 0), 0)))
        args.append(xb)
    in_specs += [pl.BlockSpec((None, _N_MOD, d), lambda i: (jnp.minimum(i // tpm, n_mod - 1), 0, 0)),
                 pl.BlockSpec((d, f), lambda i: (0, 0)),
                 pl.BlockSpec((d, f), lambda i: (0, 0)),
                 pl.BlockSpec((f, d), lambda i: (0, 0))]
    args += [modt, s1, s3, s2]
    if final_g is not None:
        in_specs.append(pl.BlockSpec((1, d), lambda i: (0, 0)))
        args.append(final_g)
    in_specs.append(pl.BlockSpec(memory_space=pl.ANY))
    args.append(ys)
    return pl.pallas_call(
        functools.partial(_combine_kernel, na=na, n_steps=n_steps, two_inputs=xb is not None,
                          final_norm=final_g is not None),
        grid=(n_steps,),
        in_specs=in_specs,
        out_specs=pl.BlockSpec((tm, d), lambda i: (i, 0)),
        out_shape=jax.ShapeDtypeStruct((n, d), _F32),
        scratch_shapes=[pltpu.VMEM((2, _TOP_K, tm, dh), _U32), pltpu.SemaphoreType.DMA((2,))],
        compiler_params=_params(("arbitrary",), disable_bounds_checks=True),
    )(*args)


def _moe(xa, na_rows, xb, g, modt, rows_per_mod, router_w, router_bias, w1, w3, w2, s1, s3, s2, final_g):
    hp, eidx8, rank8, gate8, counts = _route(xa, na_rows, xb, g, modt, rows_per_mod,
                                             router_w.T, router_bias.reshape(-1, 1), tm=256)
    counts = counts[:, 0].astype(jnp.int32)
    starts = jnp.cumsum(counts) - counts
    dest8 = starts[eidx8] + rank8
    xs = _dispatch(hp, dest8, tm=256)
    ys = _experts(xs, counts, w1, w3, w2, tmx=256)
    return _combine(ys, dest8, gate8.T, hp, xa, na_rows, xb, modt, rows_per_mod,
                    s1.astype(_BF16), s3.astype(_BF16), s2.astype(_BF16), final_g, tm=128)


def kernel(x, c, ctx, c_ctx, ada_w, ada_b, norm_g, fourier_wo, fourier_bo, da_wqkv, da_wo, da_lambda, da_subln_g, router_w, router_bias, exp_w1, exp_w3, exp_w2, shared_w1, shared_w3, shared_w2, final_g):
    b, s, d = x.shape
    n_ctx = ctx.shape[1]
    depth = ada_w.shape[0]
    assert depth == 2 and b + 1 <= 8 and s % _GRID_W == 0
    nl, nc = b * s, b * n_ctx

    cond8 = jnp.zeros((8, d), _F32).at[:b].set(c).at[b].set(c_ctx)
    mods = _adaln(cond8, ada_w, ada_b).reshape(depth, 8, _N_MOD, d)

    modt = mods[0, :b + 1]
    g_mix, g_moe = norm_g[0, 0:1], norm_g[0, 1:2]
    n2 = 64
    n1 = s // n2
    chan, m1, tw, m3 = _dft_tables(s, n1, n2, d // _FOURIER_GROUPS)
    wo = fourier_wo[0].astype(_BF16)
    bo = fourier_bo[0:1]

    yl = _normmod_chandft(x, g_mix, modt, True, chan, tm=256)
    fl = _seq_dft(yl, m1, tw, m3, n1, n2, k1b=min(4, n1))
    xl = _mm(fl.reshape(nl, d), wo, n_out=d, bias=bo, res=x.reshape(nl, d), gate=modt[:, 2:3], rows_per_gate=s,
             tm=1024, tn=512)

    yc = _normmod_chandft(ctx, g_mix, modt[b:b + 1], False, chan, tm=n_ctx)
    fc = _ctx_seq_dft(yc, tn=512)
    xc = _mm(fc.reshape(nc, d), wo, n_out=d, bias=bo, res=ctx.reshape(nc, d), gate=modt[b:b + 1, 2:3],
             rows_per_gate=nc, tm=n_ctx, tn=512)

    x_all = _moe(xl, nl, xc, g_moe, modt, s, router_w[0], router_bias[0], exp_w1[0], exp_w3[0], exp_w2[0],
                 shared_w1[0], shared_w3[0], shared_w2[0], None)

    modt = mods[1, :b + 1]
    g_mix, g_moe = norm_g[1, 0:1], norm_g[1, 1:2]
    lam_init = 0.8 - 0.6 * float(np.exp(-0.3 * 1))
    wqkv = da_wqkv[0].astype(_BF16)
    cos, sin = _rope_tables(s)

    hl = _normmod(x_all, 0, nl, g_mix, modt, s, 0, tm=512)
    hc = _normmod(x_all, nl, nc, g_mix, modt, nc, b, tm=n_ctx)
    qkv = _qkv(hl, wqkv, cos, sin, s, tm=1024, tn=512)
    kvc = _mm(hc, wqkv, n_out=2 * d, w_col0=d, out_dtype=_BF16, tm=n_ctx, tn=512)
    o = _attention(qkv.reshape(b, s, 3 * d), kvc.reshape(b, n_ctx, 2 * d), da_lambda[0], da_subln_g[0:1],
                   lam_init, tq=512, tk=512)
    xl = _mm(o.reshape(nl, d), da_wo[0].astype(_BF16), n_out=d, res=x_all, gate=modt[:, 2:3], rows_per_gate=s,
             tm=1024, tn=512)

    out = _moe(xl, nl, None, g_moe, modt, s, router_w[1], router_bias[1], exp_w1[1], exp_w3[1], exp_w2[1],
               shared_w1[1], shared_w3[1], shared_w2[1], final_g.reshape(1, d))
    return out.reshape(b, s, d)
```

```python
import functools
import math

import numpy as np
import jax
import jax.numpy as jnp
from jax import lax
from jax.experimental import pallas as pl
from jax.experimental.pallas import tpu as pltpu

_GRID_W = 64
_N_MOD = 6
_NORM_EPS = 1e-6
_FOURIER_GROUPS = 8
_HEAD_DIM = 128
_V_DIM = 2 * _HEAD_DIM
_ROPE_BASE = 10000.0
_TOP_K = 8
_N_GROUPS = 8
_TOPK_GROUPS = 4
_ROUTED_SCALE = 2.5

_LANE = 128
_VMEM_LIMIT = 56 * 1024 * 1024

_F32 = jnp.float32
_BF16 = jnp.bfloat16
_U32 = jnp.uint32
_HI_MASK = np.uint32(0xFFFF0000)
_NEG_INF = float("-inf")


def _params(sem, **kw):
    return pltpu.CompilerParams(dimension_semantics=sem, vmem_limit_bytes=_VMEM_LIMIT, **kw)


def _pack_bf16_pair(lo, hi):
    ul = pltpu.bitcast(lo.astype(_BF16).astype(_F32), _U32)
    uh = pltpu.bitcast(hi.astype(_BF16).astype(_F32), _U32)
    return (ul >> 16) | (uh & _HI_MASK)


def _unpack_lo(u):
    return pltpu.bitcast(u << 16, _F32)


def _unpack_hi(u):
    return pltpu.bitcast(u & _HI_MASK, _F32)


def _rms_mod(x, g, shift, scale):
    ms = jnp.mean(x * x, axis=-1, keepdims=True)
    return x * lax.rsqrt(ms + _NORM_EPS) * g * (1.0 + scale) + shift


def _adaln_kernel(c_ref, w_ref, b_ref, o_ref):
    c = c_ref[...]
    s = (c * jax.nn.sigmoid(c)).astype(_BF16)
    o_ref[...] = jnp.dot(s, w_ref[...].astype(_BF16), preferred_element_type=_F32) + b_ref[...]


def _adaln(cond8, ada_w, ada_b):
    depth, d, n6 = ada_w.shape
    tn = min(512, n6)
    return pl.pallas_call(
        _adaln_kernel,
        grid=(depth, n6 // tn),
        in_specs=[pl.BlockSpec((8, d), lambda l, j: (0, 0)),
                  pl.BlockSpec((None, d, tn), lambda l, j: (l, 0, j)),
                  pl.BlockSpec((None, 1, tn), lambda l, j: (l, 0, j))],
        out_specs=pl.BlockSpec((None, 8, tn), lambda l, j: (l, 0, j)),
        out_shape=jax.ShapeDtypeStruct((depth, 8, n6), _F32),
        compiler_params=_params(("arbitrary", "arbitrary")),
    )(cond8, ada_w, ada_b.reshape(depth, 1, n6))


def _normmod_chandft_kernel(x_ref, g_ref, mod_ref, tab_ref, o_ref, *, groups):
    h = _rms_mod(x_ref[...], g_ref[...], mod_ref[0:1, :], mod_ref[1:2, :]).astype(_BF16)
    cg = h.shape[-1] // groups
    for g in range(groups):
        r = jnp.dot(h[:, g * cg:(g + 1) * cg], tab_ref[...], preferred_element_type=_F32)
        o_ref[:, g * cg:(g + 1) * cg] = _pack_bf16_pair(r[:, :cg], r[:, cg:])


def _normmod_chandft(x3, g, mod, mod_per_batch, tab, tm):
    b, t, d = x3.shape
    mod_map = (lambda bi, i: (bi, 0, 0)) if mod_per_batch else (lambda bi, i: (0, 0, 0))
    return pl.pallas_call(
        functools.partial(_normmod_chandft_kernel, groups=_FOURIER_GROUPS),
        grid=(b, t // tm),
        in_specs=[pl.BlockSpec((None, tm, d), lambda bi, i: (bi, i, 0)),
                  pl.BlockSpec((1, d), lambda bi, i: (0, 0)),
                  pl.BlockSpec((None, _N_MOD, d), mod_map),
                  pl.BlockSpec(tab.shape, lambda bi, i: (0, 0))],
        out_specs=pl.BlockSpec((None, tm, d), lambda bi, i: (bi, i, 0)),
        out_shape=jax.ShapeDtypeStruct((b, t, d), _U32),
        compiler_params=_params(("arbitrary", "arbitrary")),
    )(x3, g, mod, tab)


def _normmod_kernel(x_ref, g_ref, mod_ref, o_ref):
    o_ref[...] = _rms_mod(x_ref[...], g_ref[...], mod_ref[0:1, :], mod_ref[1:2, :]).astype(_BF16)


def _normmod(x2, row0, rows, g, mod, rows_per_mod, mod0, tm):
    d = x2.shape[1]
    t0 = row0 // tm
    tpm = rows_per_mod // tm
    return pl.pallas_call(
        _normmod_kernel,
        grid=(rows // tm,),
        in_specs=[pl.BlockSpec((tm, d), lambda i: (t0 + i, 0)),
                  pl.BlockSpec((1, d), lambda i: (0, 0)),
                  pl.BlockSpec((None, _N_MOD, d), lambda i: (mod0 + i // tpm, 0, 0))],
        out_specs=pl.BlockSpec((tm, d), lambda i: (i, 0)),
        out_shape=jax.ShapeDtypeStruct((rows, d), _BF16),
        compiler_params=_params(("arbitrary",)),
    )(x2, g, mod)


def _unpack_complex_rows(u):
    return jnp.concatenate([_unpack_lo(u), _unpack_hi(u)], axis=0).astype(_BF16)


def _seq_dft_kernel(*refs, n1, n2, n_col):
    y_refs = refs[:n_col]
    m1_ref, m2_ref, o_ref, a_sc, f_sc = refs[n_col:]

    def stage1(t1, carry):
        u = jnp.concatenate([y[pl.ds(t1, n2, stride=n1), :] for y in y_refs], axis=1)
        r = jnp.dot(m1_ref[t1], _unpack_complex_rows(u), preferred_element_type=_F32)
        packed = _pack_bf16_pair(r[:n2], r[n2:])
        for c in range(n_col):
            a_sc[c, pl.ds(t1, n2, stride=n1), :] = packed[:, c * _LANE:(c + 1) * _LANE]
        return carry

    lax.fori_loop(0, n1, stage1, 0, unroll=4)

    def stage2(k2, carry):
        rows = pl.ds(pl.multiple_of(k2 * n1, n1), n1)
        u = jnp.concatenate([a_sc[c, rows, :] for c in range(n_col)], axis=1)
        r = jnp.dot(m2_ref[...], _unpack_complex_rows(u), preferred_element_type=_F32)
        for c in range(n_col):
            f_sc[c, pl.ds(k2, n1, stride=n2), :] = r[:, c * _LANE:(c + 1) * _LANE]
        return carry

    lax.fori_loop(0, n2, stage2, 0, unroll=4)
    for c in range(n_col):
        o_ref[:, c * _LANE:(c + 1) * _LANE] = f_sc[c].astype(_BF16)


def _cos_sin(n, rows, cols):
    ang = 2.0 * np.pi * ((np.arange(rows, dtype=np.int64)[:, None] * np.arange(cols, dtype=np.int64)[None, :]) % n) / n
    return np.cos(ang), np.sin(ang)


def _dft_tables(t, n1, n2, cg):
    cc, sc = _cos_sin(cg, cg, cg)
    chan = np.concatenate([cc, -sc], axis=1) / math.sqrt(cg)
    k2 = np.arange(n2, dtype=np.int64)
    idx = (k2[None, :, None] * k2[None, None, :] * n1 + k2[None, :, None] * np.arange(n1, dtype=np.int64)[:, None, None]) % t
    gr = np.cos(2.0 * np.pi * idx / t) / math.sqrt(t)
    gi = -np.sin(2.0 * np.pi * idx / t) / math.sqrt(t)
    m1 = np.concatenate([np.concatenate([gr, -gi], axis=2), np.concatenate([gi, gr], axis=2)], axis=1)
    c1, s1 = _cos_sin(n1, n1, n1)
    m2 = np.concatenate([c1, s1], axis=1)
    return jnp.asarray(chan, _BF16), jnp.asarray(m1, _BF16), jnp.asarray(m2, _BF16)


def _seq_dft(y, m1, m2, n1, n2, tn):
    b, t, d = y.shape
    n_col = tn // _LANE
    col_spec = lambda c: pl.BlockSpec((None, t, _LANE), lambda bi, j: (bi, 0, n_col * j + c))
    return pl.pallas_call(
        functools.partial(_seq_dft_kernel, n1=n1, n2=n2, n_col=n_col),
        grid=(b, d // tn),
        in_specs=[col_spec(c) for c in range(n_col)]
        + [pl.BlockSpec(m1.shape, lambda bi, j: (0, 0, 0)),
           pl.BlockSpec(m2.shape, lambda bi, j: (0, 0))],
        out_specs=pl.BlockSpec((None, t, tn), lambda bi, j: (bi, 0, j)),
        out_shape=jax.ShapeDtypeStruct((b, t, d), _BF16),
        scratch_shapes=[pltpu.VMEM((n_col, t, _LANE), _U32), pltpu.VMEM((n_col, t, _LANE), _F32)],
        compiler_params=_params(("arbitrary", "arbitrary")),
    )(*([y] * n_col), m1, m2)


def _ctx_dft_kernel(tab_ref, y_ref, o_ref):
    o_ref[...] = jnp.dot(tab_ref[...], _unpack_complex_rows(y_ref[...]), preferred_element_type=_F32).astype(_BF16)


def _ctx_seq_dft(y, tn):
    b, t, d = y.shape
    c, s = _cos_sin(t, t, t)
    tab = jnp.asarray(np.concatenate([c, s], axis=1) / math.sqrt(t), _BF16)
    return pl.pallas_call(
        _ctx_dft_kernel,
        grid=(b, d // tn),
        in_specs=[pl.BlockSpec(tab.shape, lambda bi, j: (0, 0)),
                  pl.BlockSpec((None, t, tn), lambda bi, j: (bi, 0, j))],
        out_specs=pl.BlockSpec((None, t, tn), lambda bi, j: (bi, 0, j)),
        out_shape=jax.ShapeDtypeStruct((b, t, d), _BF16),
        compiler_params=_params(("arbitrary", "arbitrary")),
    )(tab, y)


def _mm_kernel(*refs, has_bias, has_res):
    a_ref, w_ref = refs[0], refs[1]
    o_ref = refs[-1]
    acc = jnp.dot(a_ref[...], w_ref[...], preferred_element_type=_F32)
    k = 2
    if has_bias:
        acc = acc + refs[k][...]
        k += 1
    if has_res:
        acc = refs[k][...] + refs[k + 1][...] * acc
    o_ref[...] = acc.astype(o_ref.dtype)


def _mm(a, w, *, n_out, w_col0=0, bias=None, res=None, res_row0=0, gate=None, rows_per_gate=None, gate0=0,
        out_dtype=_F32, tm, tn):
    m, k = a.shape
    wj0 = w_col0 // tn
    in_specs = [pl.BlockSpec((tm, k), lambda i, j: (i, 0)),
                pl.BlockSpec((k, tn), lambda i, j: (0, wj0 + j))]
    args = [a, w]
    if bias is not None:
        in_specs.append(pl.BlockSpec((1, tn), lambda i, j: (0, j)))
        args.append(bias)
    if res is not None:
        r0 = res_row0 // tm
        tpg = rows_per_gate // tm
        in_specs.append(pl.BlockSpec((tm, tn), lambda i, j: (r0 + i, j)))
        in_specs.append(pl.BlockSpec((None, 1, tn), lambda i, j: (gate0 + i // tpg, 0, j)))
        args += [res, gate]
    return pl.pallas_call(
        functools.partial(_mm_kernel, has_bias=bias is not None, has_res=res is not None),
        grid=(m // tm, n_out // tn),
        in_specs=in_specs,
        out_specs=pl.BlockSpec((tm, tn), lambda i, j: (i, j)),
        out_shape=jax.ShapeDtypeStruct((m, n_out), out_dtype),
        compiler_params=_params(("arbitrary", "arbitrary")),
    )(*args)


def _qkv_kernel(a_ref, w_ref, cos_ref, sin_ref, o_ref, vt_ref, *, nq, nk, scale):
    j = pl.program_id(1)
    acc = jnp.dot(a_ref[...], w_ref[...], preferred_element_type=_F32)
    tn = acc.shape[1]

    @pl.when(j >= nq + nk)
    def _():
        vt_ref[...] = acc.T.astype(_BF16)

    @pl.when(j < nq + nk)
    def _():
        cos = cos_ref[...]
        sin = sin_ref[...]
        lane = lax.broadcasted_iota(jnp.int32, cos.shape, 1)
        first = (lane % (_HEAD_DIM // 2)) < (_HEAD_DIM // 4)
        sc = jnp.where(j < nq, scale, 1.0).astype(_F32)
        for u in range(tn // _HEAD_DIM):
            x = acc[:, u * _HEAD_DIM:(u + 1) * _HEAD_DIM]
            sw = jnp.where(first, pltpu.roll(x, _HEAD_DIM - _HEAD_DIM // 4, 1), pltpu.roll(x, _HEAD_DIM // 4, 1))
            o_ref[:, u * _HEAD_DIM:(u + 1) * _HEAD_DIM] = ((x * cos + sw * sin) * sc).astype(_BF16)


def _rope_tables(s):
    n_freq = _HEAD_DIM // 4
    freqs = _ROPE_BASE ** (-jnp.arange(n_freq, dtype=_F32) / n_freq)
    pos = jnp.arange(s, dtype=jnp.int32)
    ang_r = (pos // _GRID_W).astype(_F32)[:, None] * freqs[None, :]
    ang_c = (pos % _GRID_W).astype(_F32)[:, None] * freqs[None, :]
    cos = jnp.concatenate([jnp.cos(ang_r)] * 2 + [jnp.cos(ang_c)] * 2, axis=1)
    sin = jnp.concatenate([-jnp.sin(ang_r), jnp.sin(ang_r), -jnp.sin(ang_c), jnp.sin(ang_c)], axis=1)
    return cos, sin


def _qkv(a, w, cos, sin, s, tm, tn):
    m, k = a.shape
    d = w.shape[1] // 3
    tps = s // tm
    nqk = 2 * d // tn
    return pl.pallas_call(
        functools.partial(_qkv_kernel, nq=d // tn, nk=d // tn, scale=_HEAD_DIM ** -0.5 * math.log2(math.e)),
        grid=(m // tm, 3 * d // tn),
        in_specs=[pl.BlockSpec((tm, k), lambda i, j: (i, 0)),
                  pl.BlockSpec((k, tn), lambda i, j: (0, j)),
                  pl.BlockSpec((tm, _HEAD_DIM), lambda i, j: (i % tps, 0)),
                  pl.BlockSpec((tm, _HEAD_DIM), lambda i, j: (i % tps, 0))],
        out_specs=[pl.BlockSpec((tm, tn), lambda i, j: (i, jnp.minimum(j, nqk - 1))),
                   pl.BlockSpec((tn, tm), lambda i, j: (jnp.maximum(j - nqk, 0), i))],
        out_shape=[jax.ShapeDtypeStruct((m, 2 * d), _BF16), jax.ShapeDtypeStruct((d, m), _BF16)],
        compiler_params=_params(("arbitrary", "arbitrary")),
    )(a, w, cos, sin)


def _attn_kernel(q_ref, k_ref, vt_ref, kc_ref, vct_ref, lam_ref, g_ref, o_ref,
                 s0, s1, p0, p1, m0, m1, l0, l1, a0, a1, acc0, acc1, *, tk, n_chunks, lam_init):
    s_sc, p_sc, m_sc, l_sc, a_sc, acc_sc = (s0, s1), (p0, p1), (m0, m1), (l0, l1), (a0, a1), (acc0, acc1)
    q = q_ref[...]
    qs = (q[:, :_HEAD_DIM], q[:, _HEAD_DIM:])
    dn = (((1,), (1,)), ((), ()))

    def scores(kb, c):
        return lax.dot_general(kb[:, c * _HEAD_DIM:(c + 1) * _HEAD_DIM], qs[c], dn, preferred_element_type=_F32)

    def k_chunk(i):
        return k_ref[pl.ds(pl.multiple_of(i * tk, tk), tk), :]

    def vt_chunk(i):
        return vt_ref[:, pl.ds(pl.multiple_of(i * tk, tk), tk)]

    kcb = kc_ref[...]
    vctb = vct_ref[...]
    for c in range(2):
        st = scores(kcb, c)
        m = jnp.max(st, axis=0, keepdims=True)
        p = jnp.exp2(st - m)
        m_sc[c][...] = m
        l_sc[c][...] = jnp.sum(p, axis=0, keepdims=True)
        acc_sc[c][...] = jnp.dot(vctb, p.astype(_BF16), preferred_element_type=_F32)
        a_sc[c][...] = jnp.ones(a_sc[c].shape, _F32)
    p_sc[1][...] = jnp.zeros(p_sc[1].shape, _BF16)

    def softmax(c):
        st = s_sc[c][...]
        m_old = m_sc[c][...]
        m_new = jnp.maximum(m_old, jnp.max(st, axis=0, keepdims=True))
        alpha = jnp.exp2(m_old - m_new)
        p = jnp.exp2(st - m_new)
        l_sc[c][...] = alpha * l_sc[c][...] + jnp.sum(p, axis=0, keepdims=True)
        m_sc[c][...] = m_new
        a_sc[c][...] = alpha
        p_sc[c][...] = p.astype(_BF16)

    def pv(c, i):
        acc_sc[c][...] = (a_sc[c][...] * acc_sc[c][...]
                          + jnp.dot(vt_chunk(i), p_sc[c][...], preferred_element_type=_F32))

    s_sc[0][...] = scores(k_chunk(0), 0)

    def body(i, carry):
        s_sc[1][...] = scores(k_chunk(i), 1)
        softmax(0)
        pv(1, jnp.maximum(i - 1, 0))
        s_sc[0][...] = scores(k_chunk(jnp.minimum(i + 1, n_chunks - 1)), 0)
        softmax(1)
        pv(0, i)
        return carry

    lax.fori_loop(0, n_chunks, body, 0)
    pv(1, n_chunks - 1)

    lp = lam_ref[...]
    lam = (jnp.exp(jnp.sum(lp[0:1] * lp[1:2], axis=-1, keepdims=True))
           - jnp.exp(jnp.sum(lp[2:3] * lp[3:4], axis=-1, keepdims=True)) + lam_init)
    ot = acc0[...] / l0[...] - lam * (acc1[...] / l1[...])
    ms = jnp.mean(ot * ot, axis=0, keepdims=True)
    ot = ot * lax.rsqrt(ms + _NORM_EPS) * g_ref[...] * (1.0 - lam_init)
    o_ref[...] = ot.T.astype(_BF16)


def _attention(qk, vt, kvc, vct, lam_p, subln_g, lam_init, tq, tk):
    b, s, d2 = qk.shape
    d = d2 // 2
    h = d // _V_DIM
    c = kvc.shape[1]
    return pl.pallas_call(
        functools.partial(_attn_kernel, tk=tk, n_chunks=s // tk, lam_init=lam_init),
        grid=(b, h, s // tq),
        in_specs=[pl.BlockSpec((None, tq, _V_DIM), lambda bi, hi, qi: (bi, qi, hi)),
                  pl.BlockSpec((None, s, _V_DIM), lambda bi, hi, qi: (bi, 0, h + hi)),
                  pl.BlockSpec((_V_DIM, s), lambda bi, hi, qi: (hi, bi)),
                  pl.BlockSpec((None, c, _V_DIM), lambda bi, hi, qi: (bi, 0, hi)),
                  pl.BlockSpec((_V_DIM, c), lambda bi, hi, qi: (hi, bi)),
                  pl.BlockSpec(lam_p.shape, lambda bi, hi, qi: (0, 0)),
                  pl.BlockSpec((_V_DIM, 1), lambda bi, hi, qi: (0, 0))],
        out_specs=pl.BlockSpec((None, tq, _V_DIM), lambda bi, hi, qi: (bi, qi, hi)),
        out_shape=jax.ShapeDtypeStruct((b, s, d), _BF16),
        scratch_shapes=([pltpu.VMEM((tk, tq), _F32)] * 2 + [pltpu.VMEM((tk, tq), _BF16)] * 2
                        + [pltpu.VMEM((1, tq), _F32)] * 6 + [pltpu.VMEM((_V_DIM, tq), _F32)] * 2),
        compiler_params=_params(("arbitrary", "arbitrary", "arbitrary")),
    )(qk, qk, vt, kvc, vct, lam_p, subln_g.reshape(_V_DIM, 1))


def _route_kernel(*refs, na, two_inputs, n_exp):
    if two_inputs:
        xa_ref, xb_ref = refs[0], refs[1]
        refs = refs[2:]
    else:
        xa_ref, xb_ref = refs[0], None
        refs = refs[1:]
    g_ref, mod_ref, wt_ref, bias_ref, hp_ref, eidx_ref, rank_ref, gate_ref, cnt_ref, carry_sc = refs
    i = pl.program_id(0)

    @pl.when(i == 0)
    def _():
        carry_sc[...] = jnp.zeros(carry_sc.shape, _F32)

    x = xa_ref[...]
    if two_inputs:
        x = jnp.where(i < na, x, xb_ref[...])
    tm, d = x.shape
    h = _rms_mod(x, g_ref[...], mod_ref[3:4, :], mod_ref[4:5, :])
    hp_ref[...] = _pack_bf16_pair(h[:, :d // 2], h[:, d // 2:])

    h_hi = h.astype(_BF16)
    h_lo = (h - h_hi.astype(_F32)).astype(_BF16)
    w = wt_ref[...]
    w_hi = w.astype(_BF16)
    w_lo = (w - w_hi.astype(_F32)).astype(_BF16)
    dn = (((1,), (1,)), ((), ()))
    logits = (lax.dot_general(w_hi, h_hi, dn, preferred_element_type=_F32)
              + lax.dot_general(w_lo, h_hi, dn, preferred_element_type=_F32)
              + lax.dot_general(w_hi, h_lo, dn, preferred_element_type=_F32))
    scores = jax.nn.sigmoid(logits)
    sel = scores + bias_ref[...]

    pg = n_exp // _N_GROUPS
    shp = (_N_GROUPS, pg, tm)
    sel3 = sel.reshape(shp)
    scores3 = scores.reshape(shp)
    midx = lax.broadcasted_iota(jnp.int32, shp, 1).astype(_F32)
    gidx = lax.broadcasted_iota(jnp.int32, (_N_GROUPS, 1, tm), 0).astype(_F32)
    eid3 = lax.broadcasted_iota(jnp.int32, shp, 0).astype(_F32) * pg + midx

    m1 = jnp.max(sel3, axis=1, keepdims=True)
    f1 = jnp.min(jnp.where(sel3 == m1, midx, float(pg)), axis=1, keepdims=True)
    m2 = jnp.max(jnp.where(midx == f1, _NEG_INF, sel3), axis=1, keepdims=True)
    grp = m1 + m2

    gsel = jnp.zeros((_N_GROUPS, 1, tm), _F32)
    for _ in range(_TOPK_GROUPS):
        m = jnp.max(grp, axis=0, keepdims=True)
        f = jnp.min(jnp.where(grp == m, gidx, float(_N_GROUPS)), axis=0, keepdims=True)
        hit = gidx == f
        gsel = jnp.where(hit, 1.0, gsel)
        grp = jnp.where(hit, _NEG_INF, grp)
    cur = jnp.where(gsel > 0.0, sel3, _NEG_INF)

    chosen = jnp.zeros(shp, _F32)
    firsts = []
    for _ in range(_TOP_K):
        m = jnp.max(jnp.max(cur, axis=1, keepdims=True), axis=0, keepdims=True)
        f = jnp.min(jnp.min(jnp.where(cur == m, eid3, float(n_exp)), axis=1, keepdims=True), axis=0, keepdims=True)
        hit = eid3 == f
        chosen = jnp.where(hit, 1.0, chosen)
        cur = jnp.where(hit, _NEG_INF, cur)
        firsts.append(f)

    wsel = scores3 * chosen
    wsum = jnp.sum(jnp.sum(wsel, axis=1, keepdims=True), axis=0, keepdims=True)
    gates3 = wsel / wsum * _ROUTED_SCALE

    ch2 = chosen.reshape(n_exp, tm)
    upper = (lax.broadcasted_iota(jnp.int32, (tm, tm), 0) < lax.broadcasted_iota(jnp.int32, (tm, tm), 1))
    prefix = jnp.dot(ch2.astype(_BF16), jnp.where(upper, 1.0, 0.0).astype(_BF16), preferred_element_type=_F32)
    carry = carry_sc[...]
    rank3 = (carry + prefix).reshape(shp)
    new_carry = carry + jnp.sum(ch2, axis=-1, keepdims=True)
    carry_sc[...] = new_carry
    cnt_ref[...] = new_carry

    for r in range(_TOP_K):
        hit = eid3 == firsts[r]
        rk = jnp.sum(jnp.sum(jnp.where(hit, rank3, 0.0), axis=1, keepdims=True), axis=0, keepdims=True)
        gt = jnp.sum(jnp.sum(jnp.where(hit, gates3, 0.0), axis=1, keepdims=True), axis=0, keepdims=True)
        eidx_ref[r:r + 1, :] = firsts[r].reshape(1, tm).astype(jnp.int32)
        rank_ref[r:r + 1, :] = rk.reshape(1, tm).astype(jnp.int32)
        gate_ref[r:r + 1, :] = gt.reshape(1, tm)


def _route(xa, na_rows, xb, g, modt, rows_per_mod, wt, bias, tm):
    d = xa.shape[1]
    n_exp = wt.shape[0]
    na = na_rows // tm
    nb = 0 if xb is None else xb.shape[0] // tm
    n = (na + nb) * tm
    tpm = rows_per_mod // tm
    n_mod = modt.shape[0]
    in_specs = [pl.BlockSpec((tm, d), lambda i: (jnp.minimum(i, na - 1), 0))]
    args = [xa]
    if xb is not None:
        in_specs.append(pl.BlockSpec((tm, d), lambda i: (jnp.maximum(i - na, 0), 0)))
        args.append(xb)
    in_specs += [pl.BlockSpec((1, d), lambda i: (0, 0)),
                 pl.BlockSpec((None, _N_MOD, d), lambda i: (jnp.minimum(i // tpm, n_mod - 1), 0, 0)),
                 pl.BlockSpec((n_exp, d), lambda i: (0, 0)),
                 pl.BlockSpec((n_exp, 1), lambda i: (0, 0))]
    args += [g, modt, wt, bias]
    return pl.pallas_call(
        functools.partial(_route_kernel, na=na, two_inputs=xb is not None, n_exp=n_exp),
        grid=(na + nb,),
        in_specs=in_specs,
        out_specs=[pl.BlockSpec((tm, d // 2), lambda i: (i, 0)),
                   pl.BlockSpec((_TOP_K, tm), lambda i: (0, i)),
                   pl.BlockSpec((_TOP_K, tm), lambda i: (0, i)),
                   pl.BlockSpec((_TOP_K, tm), lambda i: (0, i)),
                   pl.BlockSpec((n_exp, 1), lambda i: (0, 0))],
        out_shape=[jax.ShapeDtypeStruct((n, d // 2), _U32),
                   jax.ShapeDtypeStruct((_TOP_K, n), jnp.int32),
                   jax.ShapeDtypeStruct((_TOP_K, n), jnp.int32),
                   jax.ShapeDtypeStruct((_TOP_K, n), _F32),
                   jax.ShapeDtypeStruct((n_exp, 1), _F32)],
        scratch_shapes=[pltpu.VMEM((n_exp, 1), _F32)],
        compiler_params=_params(("arbitrary",)),
    )(*args)


def _dispatch_kernel(dest_ref, hp_ref, xs_ref, sem):
    tm = hp_ref.shape[0]

    def row_copy(n, k):
        return pltpu.make_async_copy(hp_ref.at[pl.ds(n, 1), :], xs_ref.at[pl.ds(dest_ref[k, n], 1), :], sem)

    def body(n, carry):
        for k in range(_TOP_K):
            row_copy(n, k).start()
        return carry

    lax.fori_loop(0, tm, body, 0)
    for _ in range(_TOP_K):
        pltpu.make_async_copy(hp_ref, xs_ref.at[pl.ds(0, tm), :], sem).wait()


def _dispatch(hp, dest8, tm):
    n, dh = hp.shape
    return pl.pallas_call(
        _dispatch_kernel,
        grid=(n // tm,),
        in_specs=[pl.BlockSpec((_TOP_K, tm), lambda i: (0, i), memory_space=pltpu.SMEM),
                  pl.BlockSpec((tm, dh), lambda i: (i, 0))],
        out_specs=pl.BlockSpec(memory_space=pl.ANY),
        out_shape=jax.ShapeDtypeStruct((n * _TOP_K, dh), _U32),
        scratch_shapes=[pltpu.SemaphoreType.DMA(())],
        compiler_params=_params(("arbitrary",), disable_bounds_checks=True),
    )(dest8, hp)


def _swiglu_packed(xu, w1, w3, w2):
    dh = xu.shape[1]
    x_lo = _unpack_lo(xu).astype(_BF16)
    x_hi = _unpack_hi(xu).astype(_BF16)
    h1 = (jnp.dot(x_lo, w1[:dh], preferred_element_type=_F32) + jnp.dot(x_hi, w1[dh:], preferred_element_type=_F32))
    h3 = (jnp.dot(x_lo, w3[:dh], preferred_element_type=_F32) + jnp.dot(x_hi, w3[dh:], preferred_element_type=_F32))
    a = (h1 * jax.nn.sigmoid(h1) * h3).astype(_BF16)
    return jnp.dot(a, w2, preferred_element_type=_F32)


def _expert_kernel(tile_ref, exp_ref, lo_ref, hi_ref, first_ref, newexp_ref, valid_ref,
                   x_ref, w1_ref, w3_ref, w2_ref, y_ref, w1b, w3b, w2b):
    v = pl.program_id(0)

    @pl.when(newexp_ref[v] == 1)
    def _():
        w1b[...] = w1_ref[...].astype(_BF16)
        w3b[...] = w3_ref[...].astype(_BF16)
        w2b[...] = w2_ref[...].astype(_BF16)

    @pl.when(valid_ref[v] == 1)
    def _():
        xu = x_ref[...]
        tmx, dh = xu.shape
        y = _swiglu_packed(xu, w1b[...], w3b[...], w2b[...])
        packed = _pack_bf16_pair(y[:, :dh], y[:, dh:])
        rows = lax.broadcasted_iota(jnp.int32, (tmx, 1), 0)
        mine = (rows >= lo_ref[v]) & (rows < hi_ref[v])

        @pl.when(first_ref[v] == 1)
        def _():
            y_ref[...] = jnp.where(mine, packed, jnp.zeros_like(packed))

        @pl.when(first_ref[v] == 0)
        def _():
            y_ref[...] = jnp.where(mine, packed, y_ref[...])


def _cumsum_small(v):
    n = v.shape[0]
    tri = jnp.arange(n)[:, None] >= jnp.arange(n)[None, :]
    return jnp.sum(jnp.where(tri, v[None, :], 0), axis=1).astype(jnp.int32)


def _take_small(table, idx):
    hot = idx[..., None] == jnp.arange(table.shape[0], dtype=jnp.int32)
    return jnp.sum(jnp.where(hot, table, 0), axis=-1).astype(jnp.int32)


def _expert_visits(counts, n_rows, tmx):
    n_exp = counts.shape[0]
    n_tiles = n_rows // tmx
    n_vis = n_tiles + n_exp - 1
    ends = _cumsum_small(counts)
    starts = ends - counts
    t_first = starts // tmx
    t_last = jnp.where(counts > 0, (ends - 1) // tmx, t_first - 1)
    nv = t_last - t_first + 1
    v_end = _cumsum_small(nv)
    v_start = v_end - nv
    total = v_end[-1]
    v = jnp.arange(n_vis, dtype=jnp.int32)
    vc = jnp.minimum(v, total - 1)
    e = jnp.sum((v_end[None, :] <= vc[:, None]).astype(jnp.int32), axis=1)
    tile = _take_small(t_first, e) + (vc - _take_small(v_start, e))
    lo = jnp.clip(_take_small(starts, e) - tile * tmx, 0, tmx)
    hi = jnp.clip(_take_small(ends, e) - tile * tmx, 0, tmx)
    valid = (v < total).astype(jnp.int32)
    prev_tile = jnp.concatenate([jnp.full((1,), -1, jnp.int32), tile[:-1]])
    prev_e = jnp.concatenate([jnp.full((1,), -1, jnp.int32), e[:-1]])
    first = ((tile != prev_tile) & (valid == 1)).astype(jnp.int32)
    newexp = ((e != prev_e) & (valid == 1)).astype(jnp.int32)
    return tile, e, lo, hi, first, newexp, valid


def _experts(xs, counts, w1, w3, w2, tmx):
    p, dh = xs.shape
    n_exp, d, f = w1.shape
    meta = _expert_visits(counts, p, tmx)
    n_vis = meta[0].shape[0]
    grid_spec = pltpu.PrefetchScalarGridSpec(
        num_scalar_prefetch=7,
        grid=(n_vis,),
        in_specs=[pl.BlockSpec((tmx, dh), lambda v, t, e, *_: (t[v], 0)),
                  pl.BlockSpec((None, d, f), lambda v, t, e, *_: (e[v], 0, 0)),
                  pl.BlockSpec((None, d, f), lambda v, t, e, *_: (e[v], 0, 0)),
                  pl.BlockSpec((None, f, d), lambda v, t, e, *_: (e[v], 0, 0))],
        out_specs=pl.BlockSpec((tmx, dh), lambda v, t, e, *_: (t[v], 0)),
        scratch_shapes=[pltpu.VMEM((d, f), _BF16), pltpu.VMEM((d, f), _BF16), pltpu.VMEM((f, d), _BF16)],
    )
    return pl.pallas_call(
        _expert_kernel,
        grid_spec=grid_spec,
        out_shape=jax.ShapeDtypeStruct((p, dh), _U32),
        compiler_params=_params(("arbitrary",)),
    )(*meta, xs, w1, w3, w2)


def _combine_kernel(*refs, na, n_steps, two_inputs, final_norm):
    dcur_ref, dnext_ref, w_ref, hp_ref, xa_ref = refs[:5]
    refs = refs[5:]
    xb_ref = None
    if two_inputs:
        xb_ref, refs = refs[0], refs[1:]
    mod_ref, s1_ref, s3_ref, s2_ref = refs[:4]
    refs = refs[4:]
    fg_ref = None
    if final_norm:
        fg_ref, refs = refs[0], refs[1:]
    ys_ref, o_ref, gbuf, sems, ysh_sc = refs
    i = pl.program_id(0)
    slot = i % 2
    tm, dh = hp_ref.shape

    def row_copy(dref, sl, n, k):
        return pltpu.make_async_copy(ys_ref.at[pl.ds(dref[k, n], 1), :], gbuf.at[sl, k, pl.ds(n, 1), :], sems.at[sl])

    def issue(dref, sl):
        def body(n, carry):
            for k in range(_TOP_K):
                row_copy(dref, sl, n, k).start()
            return carry

        lax.fori_loop(0, tm, body, 0)

    @pl.when(i == 0)
    def _():
        issue(dcur_ref, 0)

    @pl.when(i + 1 < n_steps)
    def _():
        issue(dnext_ref, 1 - slot)

    ysh_sc[...] = _swiglu_packed(hp_ref[...], s1_ref[...], s3_ref[...], s2_ref[...])

    for k in range(_TOP_K):
        pltpu.make_async_copy(ys_ref.at[pl.ds(0, tm), :], gbuf.at[slot, k], sems.at[slot]).wait()

    def rows_body(r, carry):
        rows = pl.ds(pl.multiple_of(r * 8, 8), 8)
        acc_lo = ysh_sc[rows, :dh]
        acc_hi = ysh_sc[rows, dh:]
        for k in range(_TOP_K):
            u = gbuf[slot, k, rows, :]
            wk = w_ref[rows, k:k + 1]
            acc_lo = acc_lo + wk * _unpack_lo(u)
            acc_hi = acc_hi + wk * _unpack_hi(u)
        x = xa_ref[rows, :]
        if two_inputs:
            x = jnp.where(i < na, x, xb_ref[rows, :])
        out_lo = x[:, :dh] + mod_ref[5:6, :dh] * acc_lo
        out_hi = x[:, dh:] + mod_ref[5:6, dh:] * acc_hi
        if final_norm:
            ms = (jnp.sum(out_lo * out_lo, axis=-1, keepdims=True)
                  + jnp.sum(out_hi * out_hi, axis=-1, keepdims=True)) / (2 * dh)
            rs = lax.rsqrt(ms + _NORM_EPS)
            out_lo = out_lo * rs * fg_ref[:, :dh]
            out_hi = out_hi * rs * fg_ref[:, dh:]
        o_ref[rows, :dh] = out_lo
        o_ref[rows, dh:] = out_hi
        return carry

    lax.fori_loop(0, tm // 8, rows_body, 0)


def _combine(ys, dest8, w8t, hp, xa, na_rows, xb, modt, rows_per_mod, s1, s3, s2, final_g, tm):
    n, dh = hp.shape
    d = 2 * dh
    f = s1.shape[1]
    n_steps = n // tm
    na = na_rows // tm
    tpm = rows_per_mod // tm
    n_mod = modt.shape[0]
    smem_spec = lambda fn: pl.BlockSpec((_TOP_K, tm), fn, memory_space=pltpu.SMEM)
    in_specs = [smem_spec(lambda i: (0, i)),
                smem_spec(lambda i: (0, jnp.minimum(i + 1, n_steps - 1))),
                pl.BlockSpec((tm, _TOP_K), lambda i: (i, 0)),
                pl.BlockSpec((tm, dh), lambda i: (i, 0)),
                pl.BlockSpec((tm, d), lambda i: (jnp.minimum(i, na - 1), 0))]
    args = [dest8, dest8, w8t, hp, xa]
    if xb is not None:
        in_specs.append(pl.BlockSpec((tm, d), lambda i: (jnp.maximum(i - na, 0), 0)))
        args.append(xb)
    in_specs += [pl.BlockSpec((None, _N_MOD, d), lambda i: (jnp.minimum(i // tpm, n_mod - 1), 0, 0)),
                 pl.BlockSpec((d, f), lambda i: (0, 0)),
                 pl.BlockSpec((d, f), lambda i: (0, 0)),
                 pl.BlockSpec((f, d), lambda i: (0, 0))]
    args += [modt, s1, s3, s2]
    if final_g is not None:
        in_specs.append(pl.BlockSpec((1, d), lambda i: (0, 0)))
        args.append(final_g)
    in_specs.append(pl.BlockSpec(memory_space=pl.ANY))
    args.append(ys)
    return pl.pallas_call(
        functools.partial(_combine_kernel, na=na, n_steps=n_steps, two_inputs=xb is not None,
                          final_norm=final_g is not None),
        grid=(n_steps,),
        in_specs=in_specs,
        out_specs=pl.BlockSpec((tm, d), lambda i: (i, 0)),
        out_shape=jax.ShapeDtypeStruct((n, d), _F32),
        scratch_shapes=[pltpu.VMEM((2, _TOP_K, tm, dh), _U32), pltpu.SemaphoreType.DMA((2,)),
                        pltpu.VMEM((tm, d), _F32)],
        compiler_params=_params(("arbitrary",), disable_bounds_checks=True),
    )(*args)


def _moe(xa, na_rows, xb, g, modt, rows_per_mod, router_w, router_bias, w1, w3, w2, s1, s3, s2, final_g):
    hp, eidx8, rank8, gate8, counts = _route(xa, na_rows, xb, g, modt, rows_per_mod,
                                             router_w.T, router_bias.reshape(-1, 1), tm=256)
    counts = counts[:, 0].astype(jnp.int32)
    starts = _cumsum_small(counts) - counts
    dest8 = _take_small(starts, eidx8) + rank8
    xs = _dispatch(hp, dest8, tm=256)
    ys = _experts(xs, counts, w1, w3, w2, tmx=256)
    return _combine(ys, dest8, gate8.T, hp, xa, na_rows, xb, modt, rows_per_mod,
                    s1.astype(_BF16), s3.astype(_BF16), s2.astype(_BF16), final_g, tm=128)


def kernel(x, c, ctx, c_ctx, ada_w, ada_b, norm_g, fourier_wo, fourier_bo, da_wqkv, da_wo, da_lambda, da_subln_g, router_w, router_bias, exp_w1, exp_w3, exp_w2, shared_w1, shared_w3, shared_w2, final_g):
    b, s, d = x.shape
    n_ctx = ctx.shape[1]
    depth = ada_w.shape[0]
    assert depth == 2 and b + 1 <= 8 and s % _GRID_W == 0
    nl, nc = b * s, b * n_ctx

    cond8 = jnp.zeros((8, d), _F32).at[:b].set(c).at[b].set(c_ctx)
    mods = _adaln(cond8, ada_w, ada_b).reshape(depth, 8, _N_MOD, d)

    modt = mods[0, :b + 1]
    g_mix, g_moe = norm_g[0, 0:1], norm_g[0, 1:2]
    n2 = 64
    n1 = s // n2
    chan, m1, m2 = _dft_tables(s, n1, n2, d // _FOURIER_GROUPS)
    wo = fourier_wo[0].astype(_BF16)
    bo = fourier_bo[0:1]

    yl = _normmod_chandft(x, g_mix, modt, True, chan, tm=256)
    fl = _seq_dft(yl, m1, m2, n1, n2, tn=256)
    xl = _mm(fl.reshape(nl, d), wo, n_out=d, bias=bo, res=x.reshape(nl, d), gate=modt[:, 2:3], rows_per_gate=s,
             tm=1024, tn=512)

    yc = _normmod_chandft(ctx, g_mix, modt[b:b + 1], False, chan, tm=n_ctx)
    fc = _ctx_seq_dft(yc, tn=512)
    xc = _mm(fc.reshape(nc, d), wo, n_out=d, bias=bo, res=ctx.reshape(nc, d), gate=modt[b:b + 1, 2:3],
             rows_per_gate=nc, tm=n_ctx, tn=512)

    x_all = _moe(xl, nl, xc, g_moe, modt, s, router_w[0], router_bias[0], exp_w1[0], exp_w3[0], exp_w2[0],
                 shared_w1[0], shared_w3[0], shared_w2[0], None)

    modt = mods[1, :b + 1]
    g_mix, g_moe = norm_g[1, 0:1], norm_g[1, 1:2]
    lam_init = 0.8 - 0.6 * float(np.exp(-0.3 * 1))
    wqkv = da_wqkv[0].astype(_BF16)
    cos, sin = _rope_tables(s)

    hl = _normmod(x_all, 0, nl, g_mix, modt, s, 0, tm=512)
    hc = _normmod(x_all, nl, nc, g_mix, modt, nc, b, tm=n_ctx)
    qk, vt = _qkv(hl, wqkv, cos, sin, s, tm=1024, tn=512)
    kvc = _mm(hc, wqkv, n_out=2 * d, w_col0=d, out_dtype=_BF16, tm=n_ctx, tn=512)
    o = _attention(qk.reshape(b, s, 2 * d), vt, kvc.reshape(b, n_ctx, 2 * d), kvc[:, d:].T, da_lambda[0],
                   da_subln_g[0], lam_init, tq=1024, tk=1024)
    xl = _mm(o.reshape(nl, d), da_wo[0].astype(_BF16), n_out=d, res=x_all, gate=modt[:, 2:3], rows_per_gate=s,
             tm=1024, tn=512)

    out = _moe(xl, nl, None, g_moe, modt, s, router_w[1], router_bias[1], exp_w1[1], exp_w3[1], exp_w2[1],
               shared_w1[1], shared_w3[1], shared_w2[1], final_g.reshape(1, d))
    return out.reshape(b, s, d)
```

```python
import functools
import math

import numpy as np
import jax
import jax.numpy as jnp
from jax import lax
from jax.experimental import pallas as pl
from jax.experimental.pallas import tpu as pltpu

_GRID_W = 64
_N_MOD = 6
_NORM_EPS = 1e-6
_FOURIER_GROUPS = 8
_HEAD_DIM = 128
_V_DIM = 2 * _HEAD_DIM
_ROPE_BASE = 10000.0
_TOP_K = 8
_N_GROUPS = 8
_TOPK_GROUPS = 4
_ROUTED_SCALE = 2.5

_LANE = 128
_VMEM_LIMIT = 56 * 1024 * 1024

_F32 = jnp.float32
_BF16 = jnp.bfloat16
_U32 = jnp.uint32
_HI_MASK = np.uint32(0xFFFF0000)
_NEG_INF = float("-inf")


def _params(sem, **kw):
    return pltpu.CompilerParams(dimension_semantics=sem, vmem_limit_bytes=_VMEM_LIMIT, **kw)


def _pack_bf16_pair(lo, hi):
    ul = pltpu.bitcast(lo.astype(_BF16).astype(_F32), _U32)
    uh = pltpu.bitcast(hi.astype(_BF16).astype(_F32), _U32)
    return (ul >> 16) | (uh & _HI_MASK)


def _unpack_lo(u):
    return pltpu.bitcast(u << 16, _F32)


def _unpack_hi(u):
    return pltpu.bitcast(u & _HI_MASK, _F32)


def _rms_mod(x, g, shift, scale):
    ms = jnp.mean(x * x, axis=-1, keepdims=True)
    return x * lax.rsqrt(ms + _NORM_EPS) * g * (1.0 + scale) + shift


def _adaln_kernel(c_ref, w_ref, b_ref, o_ref):
    c = c_ref[...]
    s = (c * jax.nn.sigmoid(c)).astype(_BF16)
    o_ref[...] = jnp.dot(s, w_ref[...].astype(_BF16), preferred_element_type=_F32) + b_ref[...]


def _adaln(cond8, ada_w, ada_b):
    depth, d, n6 = ada_w.shape
    tn = min(512, n6)
    return pl.pallas_call(
        _adaln_kernel,
        grid=(depth, n6 // tn),
        in_specs=[pl.BlockSpec((8, d), lambda l, j: (0, 0)),
                  pl.BlockSpec((None, d, tn), lambda l, j: (l, 0, j)),
                  pl.BlockSpec((None, 1, tn), lambda l, j: (l, 0, j))],
        out_specs=pl.BlockSpec((None, 8, tn), lambda l, j: (l, 0, j)),
        out_shape=jax.ShapeDtypeStruct((depth, 8, n6), _F32),
        compiler_params=_params(("arbitrary", "arbitrary")),
    )(cond8, ada_w, ada_b.reshape(depth, 1, n6))


def _normmod_chandft_kernel(x_ref, g_ref, mod_ref, tab_ref, o_ref, *, groups):
    h = _rms_mod(x_ref[...], g_ref[...], mod_ref[0:1, :], mod_ref[1:2, :]).astype(_BF16)
    cg = h.shape[-1] // groups
    for g in range(groups):
        r = jnp.dot(h[:, g * cg:(g + 1) * cg], tab_ref[...], preferred_element_type=_F32)
        o_ref[:, g * cg:(g + 1) * cg] = _pack_bf16_pair(r[:, :cg], r[:, cg:])


def _normmod_chandft(x3, g, mod, mod_per_batch, tab, tm):
    b, t, d = x3.shape
    mod_map = (lambda bi, i: (bi, 0, 0)) if mod_per_batch else (lambda bi, i: (0, 0, 0))
    return pl.pallas_call(
        functools.partial(_normmod_chandft_kernel, groups=_FOURIER_GROUPS),
        grid=(b, t // tm),
        in_specs=[pl.BlockSpec((None, tm, d), lambda bi, i: (bi, i, 0)),
                  pl.BlockSpec((1, d), lambda bi, i: (0, 0)),
                  pl.BlockSpec((None, _N_MOD, d), mod_map),
                  pl.BlockSpec(tab.shape, lambda bi, i: (0, 0))],
        out_specs=pl.BlockSpec((None, tm, d), lambda bi, i: (bi, i, 0)),
        out_shape=jax.ShapeDtypeStruct((b, t, d), _U32),
        compiler_params=_params(("arbitrary", "arbitrary")),
    )(x3, g, mod, tab)


def _normmod_kernel(x_ref, g_ref, mod_ref, o_ref):
    o_ref[...] = _rms_mod(x_ref[...], g_ref[...], mod_ref[0:1, :], mod_ref[1:2, :]).astype(_BF16)


def _normmod(x2, row0, rows, g, mod, rows_per_mod, mod0, tm):
    d = x2.shape[1]
    t0 = row0 // tm
    tpm = rows_per_mod // tm
    return pl.pallas_call(
        _normmod_kernel,
        grid=(rows // tm,),
        in_specs=[pl.BlockSpec((tm, d), lambda i: (t0 + i, 0)),
                  pl.BlockSpec((1, d), lambda i: (0, 0)),
                  pl.BlockSpec((None, _N_MOD, d), lambda i: (mod0 + i // tpm, 0, 0))],
        out_specs=pl.BlockSpec((tm, d), lambda i: (i, 0)),
        out_shape=jax.ShapeDtypeStruct((rows, d), _BF16),
        compiler_params=_params(("arbitrary",)),
    )(x2, g, mod)


def _unpack_complex_rows(u):
    return jnp.concatenate([_unpack_lo(u), _unpack_hi(u)], axis=0).astype(_BF16)


def _seq_dft_kernel(*refs, n1, n2, n_col):
    y_refs = refs[:n_col]
    m1_ref, m2_ref, o_ref, a_sc, f_sc = refs[n_col:]

    def stage1(t1, carry):
        u = jnp.concatenate([y[pl.ds(t1, n2, stride=n1), :] for y in y_refs], axis=1)
        r = jnp.dot(m1_ref[t1], _unpack_complex_rows(u), preferred_element_type=_F32)
        packed = _pack_bf16_pair(r[:n2], r[n2:])
        for c in range(n_col):
            a_sc[c, pl.ds(t1, n2, stride=n1), :] = packed[:, c * _LANE:(c + 1) * _LANE]
        return carry

    lax.fori_loop(0, n1, stage1, 0, unroll=4)

    def stage2(k2, carry):
        rows = pl.ds(pl.multiple_of(k2 * n1, n1), n1)
        u = jnp.concatenate([a_sc[c, rows, :] for c in range(n_col)], axis=1)
        r = jnp.dot(m2_ref[...], _unpack_complex_rows(u), preferred_element_type=_F32)
        for c in range(n_col):
            f_sc[c, pl.ds(k2, n1, stride=n2), :] = r[:, c * _LANE:(c + 1) * _LANE]
        return carry

    lax.fori_loop(0, n2, stage2, 0, unroll=4)
    for c in range(n_col):
        o_ref[:, c * _LANE:(c + 1) * _LANE] = f_sc[c].astype(_BF16)


def _cos_sin(n, rows, cols):
    ang = 2.0 * np.pi * ((np.arange(rows, dtype=np.int64)[:, None] * np.arange(cols, dtype=np.int64)[None, :]) % n) / n
    return np.cos(ang), np.sin(ang)


def _dft_tables(t, n1, n2, cg):
    cc, sc = _cos_sin(cg, cg, cg)
    chan = np.concatenate([cc, -sc], axis=1) / math.sqrt(cg)
    k2 = np.arange(n2, dtype=np.int64)
    idx = (k2[None, :, None] * k2[None, None, :] * n1 + k2[None, :, None] * np.arange(n1, dtype=np.int64)[:, None, None]) % t
    gr = np.cos(2.0 * np.pi * idx / t) / math.sqrt(t)
    gi = -np.sin(2.0 * np.pi * idx / t) / math.sqrt(t)
    m1 = np.concatenate([np.concatenate([gr, -gi], axis=2), np.concatenate([gi, gr], axis=2)], axis=1)
    c1, s1 = _cos_sin(n1, n1, n1)
    m2 = np.concatenate([c1, s1], axis=1)
    return jnp.asarray(chan, _BF16), jnp.asarray(m1, _BF16), jnp.asarray(m2, _BF16)


def _seq_dft(y, m1, m2, n1, n2, tn):
    b, t, d = y.shape
    n_col = tn // _LANE
    col_spec = lambda c: pl.BlockSpec((None, t, _LANE), lambda bi, j: (bi, 0, n_col * j + c))
    return pl.pallas_call(
        functools.partial(_seq_dft_kernel, n1=n1, n2=n2, n_col=n_col),
        grid=(b, d // tn),
        in_specs=[col_spec(c) for c in range(n_col)]
        + [pl.BlockSpec(m1.shape, lambda bi, j: (0, 0, 0)),
           pl.BlockSpec(m2.shape, lambda bi, j: (0, 0))],
        out_specs=pl.BlockSpec((None, t, tn), lambda bi, j: (bi, 0, j)),
        out_shape=jax.ShapeDtypeStruct((b, t, d), _BF16),
        scratch_shapes=[pltpu.VMEM((n_col, t, _LANE), _U32), pltpu.VMEM((n_col, t, _LANE), _F32)],
        compiler_params=_params(("arbitrary", "arbitrary")),
    )(*([y] * n_col), m1, m2)


def _ctx_dft_kernel(tab_ref, y_ref, o_ref):
    o_ref[...] = jnp.dot(tab_ref[...], _unpack_complex_rows(y_ref[...]), preferred_element_type=_F32).astype(_BF16)


def _ctx_seq_dft(y, tn):
    b, t, d = y.shape
    c, s = _cos_sin(t, t, t)
    tab = jnp.asarray(np.concatenate([c, s], axis=1) / math.sqrt(t), _BF16)
    return pl.pallas_call(
        _ctx_dft_kernel,
        grid=(b, d // tn),
        in_specs=[pl.BlockSpec(tab.shape, lambda bi, j: (0, 0)),
                  pl.BlockSpec((None, t, tn), lambda bi, j: (bi, 0, j))],
        out_specs=pl.BlockSpec((None, t, tn), lambda bi, j: (bi, 0, j)),
        out_shape=jax.ShapeDtypeStruct((b, t, d), _BF16),
        compiler_params=_params(("arbitrary", "arbitrary")),
    )(tab, y)


def _mm_kernel(*refs, has_bias, has_res):
    a_ref, w_ref = refs[0], refs[1]
    o_ref = refs[-1]
    acc = jnp.dot(a_ref[...], w_ref[...], preferred_element_type=_F32)
    k = 2
    if has_bias:
        acc = acc + refs[k][...]
        k += 1
    if has_res:
        acc = refs[k][...] + refs[k + 1][...] * acc
    o_ref[...] = acc.astype(o_ref.dtype)


def _mm(a, w, *, n_out, w_col0=0, bias=None, res=None, res_row0=0, gate=None, rows_per_gate=None, gate0=0,
        out_dtype=_F32, tm, tn):
    m, k = a.shape
    wj0 = w_col0 // tn
    in_specs = [pl.BlockSpec((tm, k), lambda i, j: (i, 0)),
                pl.BlockSpec((k, tn), lambda i, j: (0, wj0 + j))]
    args = [a, w]
    if bias is not None:
        in_specs.append(pl.BlockSpec((1, tn), lambda i, j: (0, j)))
        args.append(bias)
    if res is not None:
        r0 = res_row0 // tm
        tpg = rows_per_gate // tm
        in_specs.append(pl.BlockSpec((tm, tn), lambda i, j: (r0 + i, j)))
        in_specs.append(pl.BlockSpec((None, 1, tn), lambda i, j: (gate0 + i // tpg, 0, j)))
        args += [res, gate]
    return pl.pallas_call(
        functools.partial(_mm_kernel, has_bias=bias is not None, has_res=res is not None),
        grid=(m // tm, n_out // tn),
        in_specs=in_specs,
        out_specs=pl.BlockSpec((tm, tn), lambda i, j: (i, j)),
        out_shape=jax.ShapeDtypeStruct((m, n_out), out_dtype),
        compiler_params=_params(("arbitrary", "arbitrary")),
    )(*args)


def _qkv_kernel(a_ref, w_ref, cos_ref, sin_ref, o_ref, vt_ref, *, nq, nk, scale):
    j = pl.program_id(1)
    acc = jnp.dot(a_ref[...], w_ref[...], preferred_element_type=_F32)
    tn = acc.shape[1]

    @pl.when(j >= nq + nk)
    def _():
        vt_ref[...] = acc.T.astype(_BF16)

    @pl.when(j < nq + nk)
    def _():
        cos = cos_ref[...]
        sin = sin_ref[...]
        lane = lax.broadcasted_iota(jnp.int32, cos.shape, 1)
        first = (lane % (_HEAD_DIM // 2)) < (_HEAD_DIM // 4)
        sc = jnp.where(j < nq, scale, 1.0).astype(_F32)
        for u in range(tn // _HEAD_DIM):
            x = acc[:, u * _HEAD_DIM:(u + 1) * _HEAD_DIM]
            sw = jnp.where(first, pltpu.roll(x, _HEAD_DIM - _HEAD_DIM // 4, 1), pltpu.roll(x, _HEAD_DIM // 4, 1))
            o_ref[:, u * _HEAD_DIM:(u + 1) * _HEAD_DIM] = ((x * cos + sw * sin) * sc).astype(_BF16)


def _rope_tables(s):
    n_freq = _HEAD_DIM // 4
    freqs = _ROPE_BASE ** (-jnp.arange(n_freq, dtype=_F32) / n_freq)
    pos = jnp.arange(s, dtype=jnp.int32)
    ang_r = (pos // _GRID_W).astype(_F32)[:, None] * freqs[None, :]
    ang_c = (pos % _GRID_W).astype(_F32)[:, None] * freqs[None, :]
    cos = jnp.concatenate([jnp.cos(ang_r)] * 2 + [jnp.cos(ang_c)] * 2, axis=1)
    sin = jnp.concatenate([-jnp.sin(ang_r), jnp.sin(ang_r), -jnp.sin(ang_c), jnp.sin(ang_c)], axis=1)
    return cos, sin


def _qkv(a, w, cos, sin, s, tm, tn):
    m, k = a.shape
    d = w.shape[1] // 3
    tps = s // tm
    nqk = 2 * d // tn
    return pl.pallas_call(
        functools.partial(_qkv_kernel, nq=d // tn, nk=d // tn, scale=_HEAD_DIM ** -0.5 * math.log2(math.e)),
        grid=(m // tm, 3 * d // tn),
        in_specs=[pl.BlockSpec((tm, k), lambda i, j: (i, 0)),
                  pl.BlockSpec((k, tn), lambda i, j: (0, j)),
                  pl.BlockSpec((tm, _HEAD_DIM), lambda i, j: (i % tps, 0)),
                  pl.BlockSpec((tm, _HEAD_DIM), lambda i, j: (i % tps, 0))],
        out_specs=[pl.BlockSpec((tm, tn), lambda i, j: (i, jnp.minimum(j, nqk - 1))),
                   pl.BlockSpec((tn, tm), lambda i, j: (jnp.maximum(j - nqk, 0), i))],
        out_shape=[jax.ShapeDtypeStruct((m, 2 * d), _BF16), jax.ShapeDtypeStruct((d, m), _BF16)],
        compiler_params=_params(("arbitrary", "arbitrary")),
    )(a, w, cos, sin)


def _attn_kernel(q_ref, k_ref, vt_ref, kc_ref, vct_ref, lam_ref, g_ref, o_ref,
                 s0, s1, p0, p1, m0, m1, l0, l1, a0, a1, acc0, acc1, *, tk, n_chunks, lam_init):
    s_sc, p_sc, m_sc, l_sc, a_sc, acc_sc = (s0, s1), (p0, p1), (m0, m1), (l0, l1), (a0, a1), (acc0, acc1)
    q = q_ref[...]
    qs = (q[:, :_HEAD_DIM], q[:, _HEAD_DIM:])
    dn = (((1,), (1,)), ((), ()))

    def scores(kb, c):
        return lax.dot_general(kb[:, c * _HEAD_DIM:(c + 1) * _HEAD_DIM], qs[c], dn, preferred_element_type=_F32)

    def k_chunk(i):
        return k_ref[pl.ds(pl.multiple_of(i * tk, tk), tk), :]

    def vt_chunk(i):
        return vt_ref[:, pl.ds(pl.multiple_of(i * tk, tk), tk)]

    kcb = kc_ref[...]
    vctb = vct_ref[...]
    st_ctx = [scores(kcb, c) for c in range(2)]
    s_sc[0][...] = scores(k_chunk(0), 0)
    for c in range(2):
        m = jnp.max(st_ctx[c], axis=0, keepdims=True)
        p = jnp.exp2(st_ctx[c] - m)
        m_sc[c][...] = m
        l_sc[c][...] = jnp.sum(p, axis=0, keepdims=True)
        acc_sc[c][...] = jnp.dot(vctb, p.astype(_BF16), preferred_element_type=_F32)
        a_sc[c][...] = jnp.ones(a_sc[c].shape, _F32)
    p_sc[1][...] = jnp.zeros(p_sc[1].shape, _BF16)

    def softmax(c):
        st = s_sc[c][...]
        m_old = m_sc[c][...]
        m_new = jnp.maximum(m_old, jnp.max(st, axis=0, keepdims=True))
        alpha = jnp.exp2(m_old - m_new)
        p = jnp.exp2(st - m_new)
        l_sc[c][...] = alpha * l_sc[c][...] + jnp.sum(p, axis=0, keepdims=True)
        m_sc[c][...] = m_new
        a_sc[c][...] = alpha
        p_sc[c][...] = p.astype(_BF16)

    def pv(c, i):
        acc_sc[c][...] = (a_sc[c][...] * acc_sc[c][...]
                          + jnp.dot(vt_chunk(i), p_sc[c][...], preferred_element_type=_F32))

    def body(i, carry):
        s_sc[1][...] = scores(k_chunk(i), 1)
        softmax(0)
        pv(1, jnp.maximum(i - 1, 0))
        s_sc[0][...] = scores(k_chunk(jnp.minimum(i + 1, n_chunks - 1)), 0)
        softmax(1)
        pv(0, i)
        return carry

    lax.fori_loop(0, n_chunks, body, 0)
    pv(1, n_chunks - 1)

    lp = lam_ref[...]
    lam = (jnp.exp(jnp.sum(lp[0:1] * lp[1:2], axis=-1, keepdims=True))
           - jnp.exp(jnp.sum(lp[2:3] * lp[3:4], axis=-1, keepdims=True)) + lam_init)
    ot = acc0[...] * (1.0 / l0[...]) - acc1[...] * (lam / l1[...])
    ms = jnp.mean(ot * ot, axis=0, keepdims=True)
    ot = ot * lax.rsqrt(ms + _NORM_EPS) * g_ref[...] * (1.0 - lam_init)
    o_ref[...] = ot.T.astype(_BF16)


def _attention(qk, vt, kvc, vct, lam_p, subln_g, lam_init, tq, tk):
    b, s, d2 = qk.shape
    d = d2 // 2
    h = d // _V_DIM
    c = kvc.shape[1]
    return pl.pallas_call(
        functools.partial(_attn_kernel, tk=tk, n_chunks=s // tk, lam_init=lam_init),
        grid=(b, h, s // tq),
        in_specs=[pl.BlockSpec((None, tq, _V_DIM), lambda bi, hi, qi: (bi, qi, hi)),
                  pl.BlockSpec((None, s, _V_DIM), lambda bi, hi, qi: (bi, 0, h + hi)),
                  pl.BlockSpec((_V_DIM, s), lambda bi, hi, qi: (hi, bi)),
                  pl.BlockSpec((None, c, _V_DIM), lambda bi, hi, qi: (bi, 0, hi)),
                  pl.BlockSpec((_V_DIM, c), lambda bi, hi, qi: (hi, bi)),
                  pl.BlockSpec(lam_p.shape, lambda bi, hi, qi: (0, 0)),
                  pl.BlockSpec((_V_DIM, 1), lambda bi, hi, qi: (0, 0))],
        out_specs=pl.BlockSpec((None, tq, _V_DIM), lambda bi, hi, qi: (bi, qi, hi)),
        out_shape=jax.ShapeDtypeStruct((b, s, d), _BF16),
        scratch_shapes=([pltpu.VMEM((tk, tq), _F32)] * 2 + [pltpu.VMEM((tk, tq), _BF16)] * 2
                        + [pltpu.VMEM((1, tq), _F32)] * 6 + [pltpu.VMEM((_V_DIM, tq), _F32)] * 2),
        compiler_params=_params(("arbitrary", "arbitrary", "arbitrary")),
    )(qk, qk, vt, kvc, vct, lam_p, subln_g.reshape(_V_DIM, 1))


def _route_kernel(*refs, na, two_inputs, n_exp):
    if two_inputs:
        xa_ref, xb_ref = refs[0], refs[1]
        refs = refs[2:]
    else:
        xa_ref, xb_ref = refs[0], None
        refs = refs[1:]
    g_ref, mod_ref, wt_ref, bias_ref, hp_ref, eidx_ref, rank_ref, gate_ref, cnt_ref, carry_sc = refs
    i = pl.program_id(0)

    @pl.when(i == 0)
    def _():
        carry_sc[...] = jnp.zeros(carry_sc.shape, _F32)

    x = xa_ref[...]
    if two_inputs:
        x = jnp.where(i < na, x, xb_ref[...])
    tm, d = x.shape
    h = _rms_mod(x, g_ref[...], mod_ref[3:4, :], mod_ref[4:5, :])
    hp_ref[...] = _pack_bf16_pair(h[:, :d // 2], h[:, d // 2:])

    h_hi = h.astype(_BF16)
    h_lo = (h - h_hi.astype(_F32)).astype(_BF16)
    w = wt_ref[...]
    w_hi = w.astype(_BF16)
    w_lo = (w - w_hi.astype(_F32)).astype(_BF16)
    dn = (((1,), (1,)), ((), ()))
    logits = (lax.dot_general(w_hi, h_hi, dn, preferred_element_type=_F32)
              + lax.dot_general(w_lo, h_hi, dn, preferred_element_type=_F32)
              + lax.dot_general(w_hi, h_lo, dn, preferred_element_type=_F32))
    scores = jax.nn.sigmoid(logits)
    sel = scores + bias_ref[...]

    pg = n_exp // _N_GROUPS
    shp = (_N_GROUPS, pg, tm)
    sel3 = sel.reshape(shp)
    scores3 = scores.reshape(shp)
    midx = lax.broadcasted_iota(jnp.int32, shp, 1).astype(_F32)
    gidx = lax.broadcasted_iota(jnp.int32, (_N_GROUPS, 1, tm), 0).astype(_F32)
    eid3 = lax.broadcasted_iota(jnp.int32, shp, 0).astype(_F32) * pg + midx

    m1 = jnp.max(sel3, axis=1, keepdims=True)
    f1 = jnp.min(jnp.where(sel3 == m1, midx, float(pg)), axis=1, keepdims=True)
    m2 = jnp.max(jnp.where(midx == f1, _NEG_INF, sel3), axis=1, keepdims=True)
    grp = m1 + m2

    gsel = jnp.zeros((_N_GROUPS, 1, tm), _F32)
    for _ in range(_TOPK_GROUPS):
        m = jnp.max(grp, axis=0, keepdims=True)
        f = jnp.min(jnp.where(grp == m, gidx, float(_N_GROUPS)), axis=0, keepdims=True)
        hit = gidx == f
        gsel = jnp.where(hit, 1.0, gsel)
        grp = jnp.where(hit, _NEG_INF, grp)
    cur = jnp.where(gsel > 0.0, sel3, _NEG_INF)

    chosen = jnp.zeros(shp, _F32)
    firsts = []
    for _ in range(_TOP_K):
        m = jnp.max(jnp.max(cur, axis=1, keepdims=True), axis=0, keepdims=True)
        f = jnp.min(jnp.min(jnp.where(cur == m, eid3, float(n_exp)), axis=1, keepdims=True), axis=0, keepdims=True)
        hit = eid3 == f
        chosen = jnp.where(hit, 1.0, chosen)
        cur = jnp.where(hit, _NEG_INF, cur)
        firsts.append(f)

    wsel = scores3 * chosen
    wsum = jnp.sum(jnp.sum(wsel, axis=1, keepdims=True), axis=0, keepdims=True)
    gates3 = wsel / wsum * _ROUTED_SCALE

    ch2 = chosen.reshape(n_exp, tm)
    upper = (lax.broadcasted_iota(jnp.int32, (tm, tm), 0) < lax.broadcasted_iota(jnp.int32, (tm, tm), 1))
    prefix = jnp.dot(ch2.astype(_BF16), jnp.where(upper, 1.0, 0.0).astype(_BF16), preferred_element_type=_F32)
    carry = carry_sc[...]
    rank3 = (carry + prefix).reshape(shp)
    new_carry = carry + jnp.sum(ch2, axis=-1, keepdims=True)
    carry_sc[...] = new_carry
    cnt_ref[...] = new_carry

    for r in range(_TOP_K):
        hit = eid3 == firsts[r]
        rk = jnp.sum(jnp.sum(jnp.where(hit, rank3, 0.0), axis=1, keepdims=True), axis=0, keepdims=True)
        gt = jnp.sum(jnp.sum(jnp.where(hit, gates3, 0.0), axis=1, keepdims=True), axis=0, keepdims=True)
        eidx_ref[r:r + 1, :] = firsts[r].reshape(1, tm).astype(jnp.int32)
        rank_ref[r:r + 1, :] = rk.reshape(1, tm).astype(jnp.int32)
        gate_ref[r:r + 1, :] = gt.reshape(1, tm)


def _route(xa, na_rows, xb, g, modt, rows_per_mod, wt, bias, tm):
    d = xa.shape[1]
    n_exp = wt.shape[0]
    na = na_rows // tm
    nb = 0 if xb is None else xb.shape[0] // tm
    n = (na + nb) * tm
    tpm = rows_per_mod // tm
    n_mod = modt.shape[0]
    in_specs = [pl.BlockSpec((tm, d), lambda i: (jnp.minimum(i, na - 1), 0))]
    args = [xa]
    if xb is not None:
        in_specs.append(pl.BlockSpec((tm, d), lambda i: (jnp.maximum(i - na, 0), 0)))
        args.append(xb)
    in_specs += [pl.BlockSpec((1, d), lambda i: (0, 0)),
                 pl.BlockSpec((None, _N_MOD, d), lambda i: (jnp.minimum(i // tpm, n_mod - 1), 0, 0)),
                 pl.BlockSpec((n_exp, d), lambda i: (0, 0)),
                 pl.BlockSpec((n_exp, 1), lambda i: (0, 0))]
    args += [g, modt, wt, bias]
    return pl.pallas_call(
        functools.partial(_route_kernel, na=na, two_inputs=xb is not None, n_exp=n_exp),
        grid=(na + nb,),
        in_specs=in_specs,
        out_specs=[pl.BlockSpec((tm, d // 2), lambda i: (i, 0)),
                   pl.BlockSpec((_TOP_K, tm), lambda i: (0, i)),
                   pl.BlockSpec((_TOP_K, tm), lambda i: (0, i)),
                   pl.BlockSpec((_TOP_K, tm), lambda i: (0, i)),
                   pl.BlockSpec((n_exp, 1), lambda i: (0, 0))],
        out_shape=[jax.ShapeDtypeStruct((n, d // 2), _U32),
                   jax.ShapeDtypeStruct((_TOP_K, n), jnp.int32),
                   jax.ShapeDtypeStruct((_TOP_K, n), jnp.int32),
                   jax.ShapeDtypeStruct((_TOP_K, n), _F32),
                   jax.ShapeDtypeStruct((n_exp, 1), _F32)],
        scratch_shapes=[pltpu.VMEM((n_exp, 1), _F32)],
        compiler_params=_params(("arbitrary",)),
    )(*args)


def _dispatch_kernel(dest_ref, hp_ref, xs_ref, sem):
    tm = hp_ref.shape[0]

    def row_copy(n, k):
        return pltpu.make_async_copy(hp_ref.at[pl.ds(n, 1), :], xs_ref.at[pl.ds(dest_ref[k, n], 1), :], sem)

    def body(n, carry):
        for k in range(_TOP_K):
            row_copy(n, k).start()
        return carry

    lax.fori_loop(0, tm, body, 0, unroll=4)
    for _ in range(_TOP_K):
        pltpu.make_async_copy(hp_ref, xs_ref.at[pl.ds(0, tm), :], sem).wait()


def _dispatch(hp, dest8, tm):
    n, dh = hp.shape
    return pl.pallas_call(
        _dispatch_kernel,
        grid=(n // tm,),
        in_specs=[pl.BlockSpec((_TOP_K, tm), lambda i: (0, i), memory_space=pltpu.SMEM),
                  pl.BlockSpec((tm, dh), lambda i: (i, 0))],
        out_specs=pl.BlockSpec(memory_space=pl.ANY),
        out_shape=jax.ShapeDtypeStruct((n * _TOP_K, dh), _U32),
        scratch_shapes=[pltpu.SemaphoreType.DMA(())],
        compiler_params=_params(("arbitrary",), disable_bounds_checks=True),
    )(dest8, hp)


def _swiglu_packed(xu, w1, w3, w2):
    dh = xu.shape[1]
    x_lo = _unpack_lo(xu).astype(_BF16)
    x_hi = _unpack_hi(xu).astype(_BF16)
    h1 = (jnp.dot(x_lo, w1[:dh], preferred_element_type=_F32) + jnp.dot(x_hi, w1[dh:], preferred_element_type=_F32))
    h3 = (jnp.dot(x_lo, w3[:dh], preferred_element_type=_F32) + jnp.dot(x_hi, w3[dh:], preferred_element_type=_F32))
    a = (h1 * jax.nn.sigmoid(h1) * h3).astype(_BF16)
    return jnp.dot(a, w2, preferred_element_type=_F32)


def _expert_kernel(tile_ref, exp_ref, lo_ref, hi_ref, first_ref, newexp_ref, valid_ref,
                   x_ref, w1_ref, w3_ref, w2_ref, y_ref, w1b, w3b, w2b):
    v = pl.program_id(0)

    @pl.when(newexp_ref[v] == 1)
    def _():
        w1b[...] = w1_ref[...].astype(_BF16)
        w3b[...] = w3_ref[...].astype(_BF16)
        w2b[...] = w2_ref[...].astype(_BF16)

    @pl.when(valid_ref[v] == 1)
    def _():
        tmx, dh = x_ref.shape
        y = _swiglu_packed(x_ref[...], w1b[...], w3b[...], w2b[...])
        packed = _pack_bf16_pair(y[:, :dh], y[:, dh:])
        rows = lax.broadcasted_iota(jnp.int32, (tmx, 1), 0)
        mine = (rows >= lo_ref[v]) & (rows < hi_ref[v])

        @pl.when(first_ref[v] == 1)
        def _():
            y_ref[...] = jnp.where(mine, packed, jnp.zeros_like(packed))

        @pl.when(first_ref[v] == 0)
        def _():
            y_ref[...] = jnp.where(mine, packed, y_ref[...])


def _cumsum_small(v):
    n = v.shape[0]
    tri = jnp.arange(n)[:, None] >= jnp.arange(n)[None, :]
    return jnp.sum(jnp.where(tri, v[None, :], 0), axis=1).astype(jnp.int32)


def _take_small(table, idx):
    hot = idx[..., None] == jnp.arange(table.shape[0], dtype=jnp.int32)
    return jnp.sum(jnp.where(hot, table, 0), axis=-1).astype(jnp.int32)


def _expert_visits(counts, n_rows, tmx):
    n_exp = counts.shape[0]
    n_tiles = n_rows // tmx
    n_vis = n_tiles + n_exp - 1
    ends = _cumsum_small(counts)
    starts = ends - counts
    t_first = starts // tmx
    t_last = jnp.where(counts > 0, (ends - 1) // tmx, t_first - 1)
    nv = t_last - t_first + 1
    v_end = _cumsum_small(nv)
    v_start = v_end - nv
    total = v_end[-1]
    v = jnp.arange(n_vis, dtype=jnp.int32)
    vc = jnp.minimum(v, total - 1)
    e = jnp.sum((v_end[None, :] <= vc[:, None]).astype(jnp.int32), axis=1)
    tile = _take_small(t_first, e) + (vc - _take_small(v_start, e))
    lo = jnp.clip(_take_small(starts, e) - tile * tmx, 0, tmx)
    hi = jnp.clip(_take_small(ends, e) - tile * tmx, 0, tmx)
    valid = (v < total).astype(jnp.int32)
    prev_tile = jnp.concatenate([jnp.full((1,), -1, jnp.int32), tile[:-1]])
    prev_e = jnp.concatenate([jnp.full((1,), -1, jnp.int32), e[:-1]])
    first = ((tile != prev_tile) & (valid == 1)).astype(jnp.int32)
    newexp = ((e != prev_e) & (valid == 1)).astype(jnp.int32)
    return tile, e, lo, hi, first, newexp, valid


def _experts(xs, counts, w1, w3, w2, layer, tmx):
    p, dh = xs.shape
    _, n_exp, d, f = w1.shape
    meta = _expert_visits(counts, p, tmx)
    n_vis = meta[0].shape[0]
    grid_spec = pltpu.PrefetchScalarGridSpec(
        num_scalar_prefetch=7,
        grid=(n_vis,),
        in_specs=[pl.BlockSpec((tmx, dh), lambda v, t, e, *_: (t[v], 0)),
                  pl.BlockSpec((None, None, d, f), lambda v, t, e, *_: (layer, e[v], 0, 0)),
                  pl.BlockSpec((None, None, d, f), lambda v, t, e, *_: (layer, e[v], 0, 0)),
                  pl.BlockSpec((None, None, f, d), lambda v, t, e, *_: (layer, e[v], 0, 0))],
        out_specs=pl.BlockSpec((tmx, dh), lambda v, t, e, *_: (t[v], 0)),
        scratch_shapes=[pltpu.VMEM((d, f), _BF16), pltpu.VMEM((d, f), _BF16), pltpu.VMEM((f, d), _BF16)],
    )
    return pl.pallas_call(
        _expert_kernel,
        grid_spec=grid_spec,
        out_shape=jax.ShapeDtypeStruct((p, dh), _U32),
        compiler_params=_params(("arbitrary",)),
    )(*meta, xs, w1, w3, w2)


def _combine_kernel(*refs, na, n_steps, two_inputs, final_norm):
    dcur_ref, dnext_ref, w_ref, hp_ref, xa_ref = refs[:5]
    refs = refs[5:]
    xb_ref = None
    if two_inputs:
        xb_ref, refs = refs[0], refs[1:]
    mod_ref, s1_ref, s3_ref, s2_ref = refs[:4]
    refs = refs[4:]
    fg_ref = None
    if final_norm:
        fg_ref, refs = refs[0], refs[1:]
    ys_ref, o_ref, gbuf, sems, ysh_sc = refs
    i = pl.program_id(0)
    slot = i % 2
    tm, dh = hp_ref.shape

    def row_copy(dref, sl, n, k):
        return pltpu.make_async_copy(ys_ref.at[pl.ds(dref[k, n], 1), :], gbuf.at[sl, k, pl.ds(n, 1), :], sems.at[sl])

    def issue(dref, sl):
        def body(n, carry):
            for k in range(_TOP_K):
                row_copy(dref, sl, n, k).start()
            return carry

        lax.fori_loop(0, tm, body, 0)

    def wait_slot(sl):
        for k in range(_TOP_K):
            pltpu.make_async_copy(ys_ref.at[pl.ds(0, tm), :], gbuf.at[sl, k], sems.at[sl]).wait()

    @pl.when(i == 0)
    def _():
        issue(dcur_ref, 0)

    ysh_sc[...] = _swiglu_packed(hp_ref[...], s1_ref[...], s3_ref[...], s2_ref[...])
    wait_slot(slot)

    def rows_body(r, carry):
        for t in range(8):
            for k in range(_TOP_K):
                row_copy(dnext_ref, 1 - slot, r * 8 + t, k).start()
        rows = pl.ds(pl.multiple_of(r * 8, 8), 8)
        acc_lo = ysh_sc[rows, :dh]
        acc_hi = ysh_sc[rows, dh:]
        for k in range(_TOP_K):
            u = gbuf[slot, k, rows, :]
            wk = w_ref[rows, k:k + 1]
            acc_lo = acc_lo + wk * _unpack_lo(u)
            acc_hi = acc_hi + wk * _unpack_hi(u)
        x = xa_ref[rows, :]
        if two_inputs:
            x = jnp.where(i < na, x, xb_ref[rows, :])
        out_lo = x[:, :dh] + mod_ref[5:6, :dh] * acc_lo
        out_hi = x[:, dh:] + mod_ref[5:6, dh:] * acc_hi
        if final_norm:
            ms = (jnp.sum(out_lo * out_lo, axis=-1, keepdims=True)
                  + jnp.sum(out_hi * out_hi, axis=-1, keepdims=True)) / (2 * dh)
            rs = lax.rsqrt(ms + _NORM_EPS)
            out_lo = out_lo * rs * fg_ref[:, :dh]
            out_hi = out_hi * rs * fg_ref[:, dh:]
        o_ref[rows, :dh] = out_lo
        o_ref[rows, dh:] = out_hi
        return carry

    lax.fori_loop(0, tm // 8, rows_body, 0)

    @pl.when(i == n_steps - 1)
    def _():
        wait_slot(1 - slot)


def _combine(ys, dest8, w8t, hp, xa, na_rows, xb, modt, rows_per_mod, s1, s3, s2, final_g, tm):
    n, dh = hp.shape
    d = 2 * dh
    f = s1.shape[1]
    n_steps = n // tm
    na = na_rows // tm
    tpm = rows_per_mod // tm
    n_mod = modt.shape[0]
    smem_spec = lambda fn: pl.BlockSpec((_TOP_K, tm), fn, memory_space=pltpu.SMEM)
    in_specs = [smem_spec(lambda i: (0, i)),
                smem_spec(lambda i: (0, jnp.minimum(i + 1, n_steps - 1))),
                pl.BlockSpec((tm, _TOP_K), lambda i: (i, 0)),
                pl.BlockSpec((tm, dh), lambda i: (i, 0)),
                pl.BlockSpec((tm, d), lambda i: (jnp.minimum(i, na - 1), 0))]
    args = [dest8, dest8, w8t, hp, xa]
    if xb is not None:
        in_specs.append(pl.BlockSpec((tm, d), lambda i: (jnp.maximum(i - na, 0), 0)))
        args.append(xb)
    in_specs += [pl.BlockSpec((None, _N_MOD, d), lambda i: (jnp.minimum(i // tpm, n_mod - 1), 0, 0)),
                 pl.BlockSpec((d, f), lambda i: (0, 0)),
                 pl.BlockSpec((d, f), lambda i: (0, 0)),
                 pl.BlockSpec((f, d), lambda i: (0, 0))]
    args += [modt, s1, s3, s2]
    if final_g is not None:
        in_specs.append(pl.BlockSpec((1, d), lambda i: (0, 0)))
        args.append(final_g)
    in_specs.append(pl.BlockSpec(memory_space=pl.ANY))
    args.append(ys)
    return pl.pallas_call(
        functools.partial(_combine_kernel, na=na, n_steps=n_steps, two_inputs=xb is not None,
                          final_norm=final_g is not None),
        grid=(n_steps,),
        in_specs=in_specs,
        out_specs=pl.BlockSpec((tm, d), lambda i: (i, 0)),
        out_shape=jax.ShapeDtypeStruct((n, d), _F32),
        scratch_shapes=[pltpu.VMEM((2, _TOP_K, tm, dh), _U32), pltpu.SemaphoreType.DMA((2,)),
                        pltpu.VMEM((tm, d), _F32)],
        compiler_params=_params(("arbitrary",), disable_bounds_checks=True),
    )(*args)


def _moe(xa, na_rows, xb, g, modt, rows_per_mod, router_w, router_bias, w1, w3, w2, layer, s1, s3, s2, final_g):
    hp, eidx8, rank8, gate8, counts = _route(xa, na_rows, xb, g, modt, rows_per_mod,
                                             router_w.T, router_bias.reshape(-1, 1), tm=256)
    counts = counts[:, 0].astype(jnp.int32)
    starts = _cumsum_small(counts) - counts
    dest8 = _take_small(starts, eidx8) + rank8
    xs = _dispatch(hp, dest8, tm=256)
    ys = _experts(xs, counts, w1, w3, w2, layer, tmx=256)
    return _combine(ys, dest8, gate8.T, hp, xa, na_rows, xb, modt, rows_per_mod,
                    s1.astype(_BF16), s3.astype(_BF16), s2.astype(_BF16), final_g, tm=128)


def kernel(x, c, ctx, c_ctx, ada_w, ada_b, norm_g, fourier_wo, fourier_bo, da_wqkv, da_wo, da_lambda, da_subln_g, router_w, router_bias, exp_w1, exp_w3, exp_w2, shared_w1, shared_w3, shared_w2, final_g):
    b, s, d = x.shape
    n_ctx = ctx.shape[1]
    depth = ada_w.shape[0]
    assert depth == 2 and b + 1 <= 8 and s % _GRID_W == 0
    nl, nc = b * s, b * n_ctx

    cond8 = jnp.zeros((8, d), _F32).at[:b].set(c).at[b].set(c_ctx)
    mods = _adaln(cond8, ada_w, ada_b).reshape(depth, 8, _N_MOD, d)

    modt = mods[0, :b + 1]
    g_mix, g_moe = norm_g[0, 0:1], norm_g[0, 1:2]
    n2 = 64
    n1 = s // n2
    chan, m1, m2 = _dft_tables(s, n1, n2, d // _FOURIER_GROUPS)
    wo = fourier_wo[0].astype(_BF16)
    bo = fourier_bo[0:1]

    yl = _normmod_chandft(x, g_mix, modt, True, chan, tm=256)
    fl = _seq_dft(yl, m1, m2, n1, n2, tn=256)
    xl = _mm(fl.reshape(nl, d), wo, n_out=d, bias=bo, res=x.reshape(nl, d), gate=modt[:, 2:3], rows_per_gate=s,
             tm=1024, tn=512)

    yc = _normmod_chandft(ctx, g_mix, modt[b:b + 1], False, chan, tm=n_ctx)
    fc = _ctx_seq_dft(yc, tn=512)
    xc = _mm(fc.reshape(nc, d), wo, n_out=d, bias=bo, res=ctx.reshape(nc, d), gate=modt[b:b + 1, 2:3],
             rows_per_gate=nc, tm=n_ctx, tn=512)

    x_all = _moe(xl, nl, xc, g_moe, modt, s, router_w[0], router_bias[0], exp_w1, exp_w3, exp_w2, 0,
                 shared_w1[0], shared_w3[0], shared_w2[0], None)

    modt = mods[1, :b + 1]
    g_mix, g_moe = norm_g[1, 0:1], norm_g[1, 1:2]
    lam_init = 0.8 - 0.6 * float(np.exp(-0.3 * 1))
    wqkv = da_wqkv[0].astype(_BF16)
    cos, sin = _rope_tables(s)

    hl = _normmod(x_all, 0, nl, g_mix, modt, s, 0, tm=512)
    hc = _normmod(x_all, nl, nc, g_mix, modt, nc, b, tm=n_ctx)
    qk, vt = _qkv(hl, wqkv, cos, sin, s, tm=1024, tn=512)
    kvc = _mm(hc, wqkv, n_out=2 * d, w_col0=d, out_dtype=_BF16, tm=n_ctx, tn=512)
    o = _attention(qk.reshape(b, s, 2 * d), vt, kvc.reshape(b, n_ctx, 2 * d), kvc[:, d:].T, da_lambda[0],
                   da_subln_g[0], lam_init, tq=1024, tk=1024)
    xl = _mm(o.reshape(nl, d), da_wo[0].astype(_BF16), n_out=d, res=x_all, gate=modt[:, 2:3], rows_per_gate=s,
             tm=1024, tn=512)

    out = _moe(xl, nl, None, g_moe, modt, s, router_w[1], router_bias[1], exp_w1, exp_w3, exp_w2, 1,
               shared_w1[1], shared_w3[1], shared_w2[1], final_g.reshape(1, d))
    return out.reshape(b, s, d)
```

```python
import functools
import math

import numpy as np
import jax
import jax.numpy as jnp
from jax import lax
from jax.experimental import pallas as pl
from jax.experimental.pallas import tpu as pltpu

_GRID_W = 64
_N_MOD = 6
_NORM_EPS = 1e-6
_FOURIER_GROUPS = 8
_HEAD_DIM = 128
_V_DIM = 2 * _HEAD_DIM
_ROPE_BASE = 10000.0
_TOP_K = 8
_N_GROUPS = 8
_TOPK_GROUPS = 4
_ROUTED_SCALE = 2.5

_LANE = 128
_VMEM_LIMIT = 56 * 1024 * 1024

_F32 = jnp.float32
_BF16 = jnp.bfloat16
_U32 = jnp.uint32
_HI_MASK = np.uint32(0xFFFF0000)
_NEG_INF = float("-inf")


def _params(sem, **kw):
    return pltpu.CompilerParams(dimension_semantics=sem, vmem_limit_bytes=_VMEM_LIMIT, **kw)


def _pack_bf16_pair(lo, hi):
    ul = pltpu.bitcast(lo.astype(_BF16).astype(_F32), _U32)
    uh = pltpu.bitcast(hi.astype(_BF16).astype(_F32), _U32)
    return (ul >> 16) | (uh & _HI_MASK)


def _unpack_lo(u):
    return pltpu.bitcast(u << 16, _F32)


def _unpack_hi(u):
    return pltpu.bitcast(u & _HI_MASK, _F32)


def _rms_mod(x, g, shift, scale):
    ms = jnp.mean(x * x, axis=-1, keepdims=True)
    return x * lax.rsqrt(ms + _NORM_EPS) * g * (1.0 + scale) + shift


def _adaln_kernel(c_ref, w_ref, b_ref, o_ref):
    c = c_ref[...]
    s = (c * jax.nn.sigmoid(c)).astype(_BF16)
    o_ref[...] = jnp.dot(s, w_ref[...].astype(_BF16), preferred_element_type=_F32) + b_ref[...]


def _adaln(cond8, ada_w, ada_b):
    depth, d, n6 = ada_w.shape
    tn = min(512, n6)
    return pl.pallas_call(
        _adaln_kernel,
        grid=(depth, n6 // tn),
        in_specs=[pl.BlockSpec((8, d), lambda l, j: (0, 0)),
                  pl.BlockSpec((None, d, tn), lambda l, j: (l, 0, j)),
                  pl.BlockSpec((None, 1, tn), lambda l, j: (l, 0, j))],
        out_specs=pl.BlockSpec((None, 8, tn), lambda l, j: (l, 0, j)),
        out_shape=jax.ShapeDtypeStruct((depth, 8, n6), _F32),
        compiler_params=_params(("arbitrary", "arbitrary")),
    )(cond8, ada_w, ada_b.reshape(depth, 1, n6))


def _normmod_chandft_kernel(x_ref, g_ref, mod_ref, tab_ref, o_ref, *, groups):
    h = _rms_mod(x_ref[...], g_ref[...], mod_ref[0:1, :], mod_ref[1:2, :]).astype(_BF16)
    cg = h.shape[-1] // groups
    for g in range(groups):
        r = jnp.dot(h[:, g * cg:(g + 1) * cg], tab_ref[...], preferred_element_type=_F32)
        o_ref[:, g * cg:(g + 1) * cg] = _pack_bf16_pair(r[:, :cg], r[:, cg:])


def _normmod_chandft(x3, g, mod, mod_per_batch, tab, tm):
    b, t, d = x3.shape
    mod_map = (lambda bi, i: (bi, 0, 0)) if mod_per_batch else (lambda bi, i: (0, 0, 0))
    return pl.pallas_call(
        functools.partial(_normmod_chandft_kernel, groups=_FOURIER_GROUPS),
        grid=(b, t // tm),
        in_specs=[pl.BlockSpec((None, tm, d), lambda bi, i: (bi, i, 0)),
                  pl.BlockSpec((1, d), lambda bi, i: (0, 0)),
                  pl.BlockSpec((None, _N_MOD, d), mod_map),
                  pl.BlockSpec(tab.shape, lambda bi, i: (0, 0))],
        out_specs=pl.BlockSpec((None, tm, d), lambda bi, i: (bi, i, 0)),
        out_shape=jax.ShapeDtypeStruct((b, t, d), _U32),
        compiler_params=_params(("arbitrary", "arbitrary")),
    )(x3, g, mod, tab)


def _normmod_kernel(x_ref, g_ref, mod_ref, o_ref):
    o_ref[...] = _rms_mod(x_ref[...], g_ref[...], mod_ref[0:1, :], mod_ref[1:2, :]).astype(_BF16)


def _normmod(x2, row0, rows, g, mod, rows_per_mod, mod0, tm):
    d = x2.shape[1]
    t0 = row0 // tm
    tpm = rows_per_mod // tm
    return pl.pallas_call(
        _normmod_kernel,
        grid=(rows // tm,),
        in_specs=[pl.BlockSpec((tm, d), lambda i: (t0 + i, 0)),
                  pl.BlockSpec((1, d), lambda i: (0, 0)),
                  pl.BlockSpec((None, _N_MOD, d), lambda i: (mod0 + i // tpm, 0, 0))],
        out_specs=pl.BlockSpec((tm, d), lambda i: (i, 0)),
        out_shape=jax.ShapeDtypeStruct((rows, d), _BF16),
        compiler_params=_params(("arbitrary",)),
    )(x2, g, mod)


def _unpack_complex_rows(u):
    return jnp.concatenate([_unpack_lo(u), _unpack_hi(u)], axis=0).astype(_BF16)


def _seq_dft_kernel(*refs, n1, n2, n_col):
    y_refs = refs[:n_col]
    m1_ref, m2_ref, o_ref, a_sc, f_sc = refs[n_col:]

    def stage1(t1, carry):
        u = jnp.concatenate([y[pl.ds(t1, n2, stride=n1), :] for y in y_refs], axis=1)
        r = jnp.dot(m1_ref[t1], _unpack_complex_rows(u), preferred_element_type=_F32)
        packed = _pack_bf16_pair(r[:n2], r[n2:])
        for c in range(n_col):
            a_sc[c, pl.ds(t1, n2, stride=n1), :] = packed[:, c * _LANE:(c + 1) * _LANE]
        return carry

    lax.fori_loop(0, n1, stage1, 0, unroll=4)

    def stage2(k2, carry):
        rows = pl.ds(pl.multiple_of(k2 * n1, n1), n1)
        u = jnp.concatenate([a_sc[c, rows, :] for c in range(n_col)], axis=1)
        r = jnp.dot(m2_ref[...], _unpack_complex_rows(u), preferred_element_type=_F32)
        for c in range(n_col):
            f_sc[c, pl.ds(k2, n1, stride=n2), :] = r[:, c * _LANE:(c + 1) * _LANE]
        return carry

    lax.fori_loop(0, n2, stage2, 0, unroll=4)
    for c in range(n_col):
        o_ref[:, c * _LANE:(c + 1) * _LANE] = f_sc[c].astype(_BF16)


def _cos_sin(n, rows, cols):
    ang = 2.0 * np.pi * ((np.arange(rows, dtype=np.int64)[:, None] * np.arange(cols, dtype=np.int64)[None, :]) % n) / n
    return np.cos(ang), np.sin(ang)


def _dft_tables(t, n1, n2, cg):
    cc, sc = _cos_sin(cg, cg, cg)
    chan = np.concatenate([cc, -sc], axis=1) / math.sqrt(cg)
    k2 = np.arange(n2, dtype=np.int64)
    idx = (k2[None, :, None] * k2[None, None, :] * n1 + k2[None, :, None] * np.arange(n1, dtype=np.int64)[:, None, None]) % t
    gr = np.cos(2.0 * np.pi * idx / t) / math.sqrt(t)
    gi = -np.sin(2.0 * np.pi * idx / t) / math.sqrt(t)
    m1 = np.concatenate([np.concatenate([gr, -gi], axis=2), np.concatenate([gi, gr], axis=2)], axis=1)
    c1, s1 = _cos_sin(n1, n1, n1)
    m2 = np.concatenate([c1, s1], axis=1)
    return jnp.asarray(chan, _BF16), jnp.asarray(m1, _BF16), jnp.asarray(m2, _BF16)


def _seq_dft(y, m1, m2, n1, n2, tn):
    b, t, d = y.shape
    n_col = tn // _LANE
    col_spec = lambda c: pl.BlockSpec((None, t, _LANE), lambda bi, j: (bi, 0, n_col * j + c))
    return pl.pallas_call(
        functools.partial(_seq_dft_kernel, n1=n1, n2=n2, n_col=n_col),
        grid=(b, d // tn),
        in_specs=[col_spec(c) for c in range(n_col)]
        + [pl.BlockSpec(m1.shape, lambda bi, j: (0, 0, 0)),
           pl.BlockSpec(m2.shape, lambda bi, j: (0, 0))],
        out_specs=pl.BlockSpec((None, t, tn), lambda bi, j: (bi, 0, j)),
        out_shape=jax.ShapeDtypeStruct((b, t, d), _BF16),
        scratch_shapes=[pltpu.VMEM((n_col, t, _LANE), _U32), pltpu.VMEM((n_col, t, _LANE), _F32)],
        compiler_params=_params(("arbitrary", "arbitrary")),
    )(*([y] * n_col), m1, m2)


def _ctx_dft_kernel(tab_ref, y_ref, o_ref):
    o_ref[...] = jnp.dot(tab_ref[...], _unpack_complex_rows(y_ref[...]), preferred_element_type=_F32).astype(_BF16)


def _ctx_seq_dft(y, tn):
    b, t, d = y.shape
    c, s = _cos_sin(t, t, t)
    tab = jnp.asarray(np.concatenate([c, s], axis=1) / math.sqrt(t), _BF16)
    return pl.pallas_call(
        _ctx_dft_kernel,
        grid=(b, d // tn),
        in_specs=[pl.BlockSpec(tab.shape, lambda bi, j: (0, 0)),
                  pl.BlockSpec((None, t, tn), lambda bi, j: (bi, 0, j))],
        out_specs=pl.BlockSpec((None, t, tn), lambda bi, j: (bi, 0, j)),
        out_shape=jax.ShapeDtypeStruct((b, t, d), _BF16),
        compiler_params=_params(("arbitrary", "arbitrary")),
    )(tab, y)


def _mm_kernel(*refs, has_bias, has_res):
    a_ref, w_ref = refs[0], refs[1]
    o_ref = refs[-1]
    acc = jnp.dot(a_ref[...], w_ref[...], preferred_element_type=_F32)
    k = 2
    if has_bias:
        acc = acc + refs[k][...]
        k += 1
    if has_res:
        acc = refs[k][...] + refs[k + 1][...] * acc
    o_ref[...] = acc.astype(o_ref.dtype)


def _mm(a, w, *, n_out, w_col0=0, bias=None, res=None, res_row0=0, gate=None, rows_per_gate=None, gate0=0,
        out_dtype=_F32, tm, tn):
    m, k = a.shape
    wj0 = w_col0 // tn
    in_specs = [pl.BlockSpec((tm, k), lambda i, j: (i, 0)),
                pl.BlockSpec((k, tn), lambda i, j: (0, wj0 + j))]
    args = [a, w]
    if bias is not None:
        in_specs.append(pl.BlockSpec((1, tn), lambda i, j: (0, j)))
        args.append(bias)
    if res is not None:
        r0 = res_row0 // tm
        tpg = rows_per_gate // tm
        in_specs.append(pl.BlockSpec((tm, tn), lambda i, j: (r0 + i, j)))
        in_specs.append(pl.BlockSpec((None, 1, tn), lambda i, j: (gate0 + i // tpg, 0, j)))
        args += [res, gate]
    return pl.pallas_call(
        functools.partial(_mm_kernel, has_bias=bias is not None, has_res=res is not None),
        grid=(m // tm, n_out // tn),
        in_specs=in_specs,
        out_specs=pl.BlockSpec((tm, tn), lambda i, j: (i, j)),
        out_shape=jax.ShapeDtypeStruct((m, n_out), out_dtype),
        compiler_params=_params(("arbitrary", "arbitrary")),
    )(*args)


def _qkv_kernel(a_ref, w_ref, cos_ref, sin_ref, o_ref, vt_ref, *, nq, nk, scale):
    j = pl.program_id(1)
    acc = jnp.dot(a_ref[...], w_ref[...], preferred_element_type=_F32)
    tn = acc.shape[1]

    @pl.when(j >= nq + nk)
    def _():
        vt_ref[...] = acc.T.astype(_BF16)

    @pl.when(j < nq + nk)
    def _():
        cos = cos_ref[...]
        sin = sin_ref[...]
        lane = lax.broadcasted_iota(jnp.int32, cos.shape, 1)
        first = (lane % (_HEAD_DIM // 2)) < (_HEAD_DIM // 4)
        sc = jnp.where(j < nq, scale, 1.0).astype(_F32)
        for u in range(tn // _HEAD_DIM):
            x = acc[:, u * _HEAD_DIM:(u + 1) * _HEAD_DIM]
            sw = jnp.where(first, pltpu.roll(x, _HEAD_DIM - _HEAD_DIM // 4, 1), pltpu.roll(x, _HEAD_DIM // 4, 1))
            o_ref[:, u * _HEAD_DIM:(u + 1) * _HEAD_DIM] = ((x * cos + sw * sin) * sc).astype(_BF16)


def _rope_tables(s):
    n_freq = _HEAD_DIM // 4
    freqs = _ROPE_BASE ** (-jnp.arange(n_freq, dtype=_F32) / n_freq)
    pos = jnp.arange(s, dtype=jnp.int32)
    ang_r = (pos // _GRID_W).astype(_F32)[:, None] * freqs[None, :]
    ang_c = (pos % _GRID_W).astype(_F32)[:, None] * freqs[None, :]
    cos = jnp.concatenate([jnp.cos(ang_r)] * 2 + [jnp.cos(ang_c)] * 2, axis=1)
    sin = jnp.concatenate([-jnp.sin(ang_r), jnp.sin(ang_r), -jnp.sin(ang_c), jnp.sin(ang_c)], axis=1)
    return cos, sin


def _qkv(a, w, cos, sin, s, tm, tn):
    m, k = a.shape
    d = w.shape[1] // 3
    tps = s // tm
    nqk = 2 * d // tn
    return pl.pallas_call(
        functools.partial(_qkv_kernel, nq=d // tn, nk=d // tn, scale=_HEAD_DIM ** -0.5 * math.log2(math.e)),
        grid=(m // tm, 3 * d // tn),
        in_specs=[pl.BlockSpec((tm, k), lambda i, j: (i, 0)),
                  pl.BlockSpec((k, tn), lambda i, j: (0, j)),
                  pl.BlockSpec((tm, _HEAD_DIM), lambda i, j: (i % tps, 0)),
                  pl.BlockSpec((tm, _HEAD_DIM), lambda i, j: (i % tps, 0))],
        out_specs=[pl.BlockSpec((tm, tn), lambda i, j: (i, jnp.minimum(j, nqk - 1))),
                   pl.BlockSpec((tn, tm), lambda i, j: (jnp.maximum(j - nqk, 0), i))],
        out_shape=[jax.ShapeDtypeStruct((m, 2 * d), _BF16), jax.ShapeDtypeStruct((d, m), _BF16)],
        compiler_params=_params(("arbitrary", "arbitrary")),
    )(a, w, cos, sin)


def _attn_kernel(q_ref, k_ref, vt_ref, kc_ref, vct_ref, lam_ref, g_ref, o_ref,
                 s0, s1, p0, p1, m0, m1, a0, a1, x0, x1, acc0, acc1, *, tk, n_chunks, lam_init):
    s_sc, p_sc, m_sc, a_sc, x_sc, acc_sc = (s0, s1), (p0, p1), (m0, m1), (a0, a1), (x0, x1), (acc0, acc1)
    q = q_ref[...]
    qs = (q[:, :_HEAD_DIM], q[:, _HEAD_DIM:])
    dn = (((1,), (1,)), ((), ()))

    def scores(kb, c):
        return lax.dot_general(kb[:, c * _HEAD_DIM:(c + 1) * _HEAD_DIM], qs[c], dn, preferred_element_type=_F32)

    def scores_to_scratch(kb, c):
        st = scores(kb, c)
        s_sc[c][...] = st
        x_sc[c][...] = jnp.max(st, axis=0, keepdims=True)

    def chunk(i):
        return pl.ds(i * tk, tk) if isinstance(i, int) else pl.ds(pl.multiple_of(i * tk, tk), tk)

    def k_chunk(i):
        return k_ref[chunk(i), :]

    def vt_chunk(i):
        return vt_ref[:, chunk(i)]

    dv = vt_ref.shape[0]

    def with_ones(vtb):
        return jnp.concatenate([vtb, jnp.ones((16, vtb.shape[1]), _BF16)], axis=0)

    kcb = kc_ref[...]
    vctb = with_ones(vct_ref[...])
    st_ctx = [scores(kcb, c) for c in range(2)]
    scores_to_scratch(k_chunk(0), 0)
    for c in range(2):
        m = jnp.max(st_ctx[c], axis=0, keepdims=True)
        p = jnp.exp2(st_ctx[c] - m)
        m_sc[c][...] = m
        acc_sc[c][...] = jnp.dot(vctb, p.astype(_BF16), preferred_element_type=_F32)
        a_sc[c][...] = jnp.ones(a_sc[c].shape, _F32)
    p_sc[1][...] = jnp.zeros(p_sc[1].shape, _BF16)

    def softmax(c):
        m_old = m_sc[c][...]
        m_new = jnp.maximum(m_old, x_sc[c][...])
        a_sc[c][...] = jnp.exp2(m_old - m_new)
        m_sc[c][...] = m_new
        p_sc[c][...] = jnp.exp2(s_sc[c][...] - m_new).astype(_BF16)

    def pv(c, i):
        acc_sc[c][...] = (a_sc[c][...] * acc_sc[c][...]
                          + jnp.dot(with_ones(vt_chunk(i)), p_sc[c][...], preferred_element_type=_F32))

    def body(i, carry):
        scores_to_scratch(k_chunk(i), 1)
        softmax(0)
        pv(1, jnp.maximum(i - 1, 0))
        scores_to_scratch(k_chunk(i + 1), 0)
        softmax(1)
        pv(0, i)
        return carry

    lax.fori_loop(0, n_chunks - 1, body, 0)
    last = n_chunks - 1
    scores_to_scratch(k_chunk(last), 1)
    softmax(0)
    pv(1, max(last - 1, 0))
    softmax(1)
    pv(0, last)
    pv(1, last)

    lp = lam_ref[...]
    lam = (jnp.exp(jnp.sum(lp[0:1] * lp[1:2], axis=-1, keepdims=True))
           - jnp.exp(jnp.sum(lp[2:3] * lp[3:4], axis=-1, keepdims=True)) + lam_init)
    ot = acc0[:dv, :] * (1.0 / acc0[dv:dv + 1, :]) - acc1[:dv, :] * (lam / acc1[dv:dv + 1, :])
    ms = jnp.mean(ot * ot, axis=0, keepdims=True)
    ot = ot * lax.rsqrt(ms + _NORM_EPS) * g_ref[...] * (1.0 - lam_init)
    o_ref[...] = ot.T.astype(_BF16)


def _attention(qk, vt, kvc, vct, lam_p, subln_g, lam_init, tq, tk):
    b, s, d2 = qk.shape
    d = d2 // 2
    h = d // _V_DIM
    c = kvc.shape[1]
    return pl.pallas_call(
        functools.partial(_attn_kernel, tk=tk, n_chunks=s // tk, lam_init=lam_init),
        grid=(b, h, s // tq),
        in_specs=[pl.BlockSpec((None, tq, _V_DIM), lambda bi, hi, qi: (bi, qi, hi)),
                  pl.BlockSpec((None, s, _V_DIM), lambda bi, hi, qi: (bi, 0, h + hi)),
                  pl.BlockSpec((_V_DIM, s), lambda bi, hi, qi: (hi, bi)),
                  pl.BlockSpec((None, c, _V_DIM), lambda bi, hi, qi: (bi, 0, hi)),
                  pl.BlockSpec((_V_DIM, c), lambda bi, hi, qi: (hi, bi)),
                  pl.BlockSpec(lam_p.shape, lambda bi, hi, qi: (0, 0)),
                  pl.BlockSpec((_V_DIM, 1), lambda bi, hi, qi: (0, 0))],
        out_specs=pl.BlockSpec((None, tq, _V_DIM), lambda bi, hi, qi: (bi, qi, hi)),
        out_shape=jax.ShapeDtypeStruct((b, s, d), _BF16),
        scratch_shapes=([pltpu.VMEM((tk, tq), _F32)] * 2 + [pltpu.VMEM((tk, tq), _BF16)] * 2
                        + [pltpu.VMEM((1, tq), _F32)] * 6 + [pltpu.VMEM((_V_DIM + 16, tq), _F32)] * 2),
        compiler_params=_params(("arbitrary", "arbitrary", "arbitrary")),
    )(qk, qk, vt, kvc, vct, lam_p, subln_g.reshape(_V_DIM, 1))


def _route_kernel(*refs, na, two_inputs, n_exp):
    if two_inputs:
        xa_ref, xb_ref = refs[0], refs[1]
        refs = refs[2:]
    else:
        xa_ref, xb_ref = refs[0], None
        refs = refs[1:]
    g_ref, mod_ref, wt_ref, bias_ref, hp_ref, eidx_ref, rank_ref, gate_ref, cnt_ref, carry_sc = refs
    i = pl.program_id(0)

    @pl.when(i == 0)
    def _():
        carry_sc[...] = jnp.zeros(carry_sc.shape, _F32)

    x = xa_ref[...]
    if two_inputs:
        x = jnp.where(i < na, x, xb_ref[...])
    tm, d = x.shape
    h = _rms_mod(x, g_ref[...], mod_ref[3:4, :], mod_ref[4:5, :])
    hp_ref[...] = _pack_bf16_pair(h[:, :d // 2], h[:, d // 2:])

    h_hi = h.astype(_BF16)
    h_lo = (h - h_hi.astype(_F32)).astype(_BF16)
    w = wt_ref[...]
    w_hi = w.astype(_BF16)
    w_lo = (w - w_hi.astype(_F32)).astype(_BF16)
    dn = (((1,), (1,)), ((), ()))
    logits = (lax.dot_general(w_hi, h_hi, dn, preferred_element_type=_F32)
              + lax.dot_general(w_lo, h_hi, dn, preferred_element_type=_F32)
              + lax.dot_general(w_hi, h_lo, dn, preferred_element_type=_F32))
    scores = jax.nn.sigmoid(logits)
    sel = scores + bias_ref[...]

    pg = n_exp // _N_GROUPS
    shp = (_N_GROUPS, pg, tm)
    sel3 = sel.reshape(shp)
    scores3 = scores.reshape(shp)
    midx = lax.broadcasted_iota(jnp.int32, shp, 1).astype(_F32)
    gidx = lax.broadcasted_iota(jnp.int32, (_N_GROUPS, 1, tm), 0).astype(_F32)
    eid3 = lax.broadcasted_iota(jnp.int32, shp, 0).astype(_F32) * pg + midx

    m1 = jnp.max(sel3, axis=1, keepdims=True)
    f1 = jnp.min(jnp.where(sel3 == m1, midx, float(pg)), axis=1, keepdims=True)
    m2 = jnp.max(jnp.where(midx == f1, _NEG_INF, sel3), axis=1, keepdims=True)
    grp = m1 + m2

    gsel = jnp.zeros((_N_GROUPS, 1, tm), _F32)
    for _ in range(_TOPK_GROUPS):
        m = jnp.max(grp, axis=0, keepdims=True)
        f = jnp.min(jnp.where(grp == m, gidx, float(_N_GROUPS)), axis=0, keepdims=True)
        hit = gidx == f
        gsel = jnp.where(hit, 1.0, gsel)
        grp = jnp.where(hit, _NEG_INF, grp)
    cur = jnp.where(gsel > 0.0, sel3, _NEG_INF)

    chosen = jnp.zeros(shp, _F32)
    firsts = []
    for _ in range(_TOP_K):
        m = jnp.max(jnp.max(cur, axis=1, keepdims=True), axis=0, keepdims=True)
        f = jnp.min(jnp.min(jnp.where(cur == m, eid3, float(n_exp)), axis=1, keepdims=True), axis=0, keepdims=True)
        hit = eid3 == f
        chosen = jnp.where(hit, 1.0, chosen)
        cur = jnp.where(hit, _NEG_INF, cur)
        firsts.append(f)

    wsel = scores3 * chosen
    wsum = jnp.sum(jnp.sum(wsel, axis=1, keepdims=True), axis=0, keepdims=True)
    gates3 = wsel / wsum * _ROUTED_SCALE

    ch2 = chosen.reshape(n_exp, tm)
    upper = (lax.broadcasted_iota(jnp.int32, (tm, tm), 0) < lax.broadcasted_iota(jnp.int32, (tm, tm), 1))
    prefix = jnp.dot(ch2.astype(_BF16), jnp.where(upper, 1.0, 0.0).astype(_BF16), preferred_element_type=_F32)
    carry = carry_sc[...]
    rank3 = (carry + prefix).reshape(shp)
    new_carry = carry + jnp.sum(ch2, axis=-1, keepdims=True)
    carry_sc[...] = new_carry
    cnt_ref[...] = new_carry

    for r in range(_TOP_K):
        hit = eid3 == firsts[r]
        rk = jnp.sum(jnp.sum(jnp.where(hit, rank3, 0.0), axis=1, keepdims=True), axis=0, keepdims=True)
        gt = jnp.sum(jnp.sum(jnp.where(hit, gates3, 0.0), axis=1, keepdims=True), axis=0, keepdims=True)
        eidx_ref[r:r + 1, :] = firsts[r].reshape(1, tm).astype(jnp.int32)
        rank_ref[r:r + 1, :] = rk.reshape(1, tm).astype(jnp.int32)
        gate_ref[r:r + 1, :] = gt.reshape(1, tm)


def _route(xa, na_rows, xb, g, modt, rows_per_mod, wt, bias, tm):
    d = xa.shape[1]
    n_exp = wt.shape[0]
    na = na_rows // tm
    nb = 0 if xb is None else xb.shape[0] // tm
    n = (na + nb) * tm
    tpm = rows_per_mod // tm
    n_mod = modt.shape[0]
    in_specs = [pl.BlockSpec((tm, d), lambda i: (jnp.minimum(i, na - 1), 0))]
    args = [xa]
    if xb is not None:
        in_specs.append(pl.BlockSpec((tm, d), lambda i: (jnp.maximum(i - na, 0), 0)))
        args.append(xb)
    in_specs += [pl.BlockSpec((1, d), lambda i: (0, 0)),
                 pl.BlockSpec((None, _N_MOD, d), lambda i: (jnp.minimum(i // tpm, n_mod - 1), 0, 0)),
                 pl.BlockSpec((n_exp, d), lambda i: (0, 0)),
                 pl.BlockSpec((n_exp, 1), lambda i: (0, 0))]
    args += [g, modt, wt, bias]
    return pl.pallas_call(
        functools.partial(_route_kernel, na=na, two_inputs=xb is not None, n_exp=n_exp),
        grid=(na + nb,),
        in_specs=in_specs,
        out_specs=[pl.BlockSpec((tm, d // 2), lambda i: (i, 0)),
                   pl.BlockSpec((_TOP_K, tm), lambda i: (0, i)),
                   pl.BlockSpec((_TOP_K, tm), lambda i: (0, i)),
                   pl.BlockSpec((_TOP_K, tm), lambda i: (0, i)),
                   pl.BlockSpec((n_exp, 1), lambda i: (0, 0))],
        out_shape=[jax.ShapeDtypeStruct((n, d // 2), _U32),
                   jax.ShapeDtypeStruct((_TOP_K, n), jnp.int32),
                   jax.ShapeDtypeStruct((_TOP_K, n), jnp.int32),
                   jax.ShapeDtypeStruct((_TOP_K, n), _F32),
                   jax.ShapeDtypeStruct((n_exp, 1), _F32)],
        scratch_shapes=[pltpu.VMEM((n_exp, 1), _F32)],
        compiler_params=_params(("arbitrary",)),
    )(*args)


def _dispatch_kernel(dest_ref, hp_ref, xs_ref, sem):
    tm = hp_ref.shape[0]

    def row_copy(n, k):
        return pltpu.make_async_copy(hp_ref.at[pl.ds(n, 1), :], xs_ref.at[pl.ds(dest_ref[k, n], 1), :], sem)

    def body(n, carry):
        for k in range(_TOP_K):
            row_copy(n, k).start()
        return carry

    lax.fori_loop(0, tm, body, 0, unroll=4)
    for _ in range(_TOP_K):
        pltpu.make_async_copy(hp_ref, xs_ref.at[pl.ds(0, tm), :], sem).wait()


def _dispatch(hp, dest8, tm):
    n, dh = hp.shape
    return pl.pallas_call(
        _dispatch_kernel,
        grid=(n // tm,),
        in_specs=[pl.BlockSpec((_TOP_K, tm), lambda i: (0, i), memory_space=pltpu.SMEM),
                  pl.BlockSpec((tm, dh), lambda i: (i, 0))],
        out_specs=pl.BlockSpec(memory_space=pl.ANY),
        out_shape=jax.ShapeDtypeStruct((n * _TOP_K, dh), _U32),
        scratch_shapes=[pltpu.SemaphoreType.DMA(())],
        compiler_params=_params(("arbitrary",), disable_bounds_checks=True),
    )(dest8, hp)


def _swiglu_packed(xu, w1, w3, w2):
    dh = xu.shape[1]
    x_lo = _unpack_lo(xu).astype(_BF16)
    x_hi = _unpack_hi(xu).astype(_BF16)
    h1 = (jnp.dot(x_lo, w1[:dh], preferred_element_type=_F32) + jnp.dot(x_hi, w1[dh:], preferred_element_type=_F32))
    h3 = (jnp.dot(x_lo, w3[:dh], preferred_element_type=_F32) + jnp.dot(x_hi, w3[dh:], preferred_element_type=_F32))
    a = (h1 * jax.nn.sigmoid(h1) * h3).astype(_BF16)
    return jnp.dot(a, w2, preferred_element_type=_F32)


def _expert_kernel(tile_ref, exp_ref, lo_ref, hi_ref, first_ref, newexp_ref, valid_ref,
                   x_ref, w1_ref, w3_ref, w2_ref, y_ref, w1b, w3b, w2b):
    v = pl.program_id(0)

    @pl.when(newexp_ref[v] == 1)
    def _():
        w1b[...] = w1_ref[...].astype(_BF16)
        w3b[...] = w3_ref[...].astype(_BF16)
        w2b[...] = w2_ref[...].astype(_BF16)

    @pl.when(valid_ref[v] == 1)
    def _():
        tmx, dh = x_ref.shape
        y = _swiglu_packed(x_ref[...], w1b[...], w3b[...], w2b[...])
        packed = _pack_bf16_pair(y[:, :dh], y[:, dh:])
        rows = lax.broadcasted_iota(jnp.int32, (tmx, 1), 0)
        mine = (rows >= lo_ref[v]) & (rows < hi_ref[v])

        @pl.when(first_ref[v] == 1)
        def _():
            y_ref[...] = jnp.where(mine, packed, jnp.zeros_like(packed))

        @pl.when(first_ref[v] == 0)
        def _():
            y_ref[...] = jnp.where(mine, packed, y_ref[...])


def _cumsum_small(v):
    n = v.shape[0]
    tri = jnp.arange(n)[:, None] >= jnp.arange(n)[None, :]
    return jnp.sum(jnp.where(tri, v[None, :], 0), axis=1).astype(jnp.int32)


def _take_small(table, idx):
    hot = idx[..., None] == jnp.arange(table.shape[0], dtype=jnp.int32)
    return jnp.sum(jnp.where(hot, table, 0), axis=-1).astype(jnp.int32)


def _expert_visits(counts, n_rows, tmx):
    n_exp = counts.shape[0]
    n_tiles = n_rows // tmx
    n_vis = n_tiles + n_exp - 1
    ends = _cumsum_small(counts)
    starts = ends - counts
    t_first = starts // tmx
    t_last = jnp.where(counts > 0, (ends - 1) // tmx, t_first - 1)
    nv = t_last - t_first + 1
    v_end = _cumsum_small(nv)
    v_start = v_end - nv
    total = v_end[-1]
    v = jnp.arange(n_vis, dtype=jnp.int32)
    vc = jnp.minimum(v, total - 1)
    e = jnp.sum((v_end[None, :] <= vc[:, None]).astype(jnp.int32), axis=1)
    tile = _take_small(t_first, e) + (vc - _take_small(v_start, e))
    lo = jnp.clip(_take_small(starts, e) - tile * tmx, 0, tmx)
    hi = jnp.clip(_take_small(ends, e) - tile * tmx, 0, tmx)
    valid = (v < total).astype(jnp.int32)
    prev_tile = jnp.concatenate([jnp.full((1,), -1, jnp.int32), tile[:-1]])
    prev_e = jnp.concatenate([jnp.full((1,), -1, jnp.int32), e[:-1]])
    first = ((tile != prev_tile) & (valid == 1)).astype(jnp.int32)
    newexp = ((e != prev_e) & (valid == 1)).astype(jnp.int32)
    return tile, e, lo, hi, first, newexp, valid


def _experts(xs, counts, w1, w3, w2, layer, tmx):
    p, dh = xs.shape
    _, n_exp, d, f = w1.shape
    meta = _expert_visits(counts, p, tmx)
    n_vis = meta[0].shape[0]
    grid_spec = pltpu.PrefetchScalarGridSpec(
        num_scalar_prefetch=7,
        grid=(n_vis,),
        in_specs=[pl.BlockSpec((tmx, dh), lambda v, t, e, *_: (t[v], 0)),
                  pl.BlockSpec((None, None, d, f), lambda v, t, e, *_: (layer, e[v], 0, 0)),
                  pl.BlockSpec((None, None, d, f), lambda v, t, e, *_: (layer, e[v], 0, 0)),
                  pl.BlockSpec((None, None, f, d), lambda v, t, e, *_: (layer, e[v], 0, 0))],
        out_specs=pl.BlockSpec((tmx, dh), lambda v, t, e, *_: (t[v], 0)),
        scratch_shapes=[pltpu.VMEM((d, f), _BF16), pltpu.VMEM((d, f), _BF16), pltpu.VMEM((f, d), _BF16)],
    )
    return pl.pallas_call(
        _expert_kernel,
        grid_spec=grid_spec,
        out_shape=jax.ShapeDtypeStruct((p, dh), _U32),
        compiler_params=_params(("arbitrary",)),
    )(*meta, xs, w1, w3, w2)


def _combine_kernel(*refs, na, n_steps, two_inputs, final_norm):
    dcur_ref, dnext_ref, w_ref, hp_ref, xa_ref = refs[:5]
    refs = refs[5:]
    xb_ref = None
    if two_inputs:
        xb_ref, refs = refs[0], refs[1:]
    mod_ref, s1_ref, s3_ref, s2_ref = refs[:4]
    refs = refs[4:]
    fg_ref = None
    if final_norm:
        fg_ref, refs = refs[0], refs[1:]
    ys_ref, o_ref, gbuf, sems, ysh_sc = refs
    i = pl.program_id(0)
    slot = i % 2
    tm, dh = hp_ref.shape

    def row_copy(dref, sl, n, k):
        return pltpu.make_async_copy(ys_ref.at[pl.ds(dref[k, n], 1), :], gbuf.at[sl, k, pl.ds(n, 1), :], sems.at[sl])

    def issue(dref, sl):
        def body(n, carry):
            for k in range(_TOP_K):
                row_copy(dref, sl, n, k).start()
            return carry

        lax.fori_loop(0, tm, body, 0)

    def wait_slot(sl):
        for k in range(_TOP_K):
            pltpu.make_async_copy(ys_ref.at[pl.ds(0, tm), :], gbuf.at[sl, k], sems.at[sl]).wait()

    @pl.when(i == 0)
    def _():
        issue(dcur_ref, 0)

    ysh_sc[...] = _swiglu_packed(hp_ref[...], s1_ref[...], s3_ref[...], s2_ref[...])
    wait_slot(slot)

    def rows_body(r, carry):
        for t in range(8):
            for k in range(_TOP_K):
                row_copy(dnext_ref, 1 - slot, r * 8 + t, k).start()
        rows = pl.ds(pl.multiple_of(r * 8, 8), 8)
        acc_lo = ysh_sc[rows, :dh]
        acc_hi = ysh_sc[rows, dh:]
        for k in range(_TOP_K):
            u = gbuf[slot, k, rows, :]
            wk = w_ref[rows, k:k + 1]
            acc_lo = acc_lo + wk * _unpack_lo(u)
            acc_hi = acc_hi + wk * _unpack_hi(u)
        x = xa_ref[rows, :]
        if two_inputs:
            x = jnp.where(i < na, x, xb_ref[rows, :])
        out_lo = x[:, :dh] + mod_ref[5:6, :dh] * acc_lo
        out_hi = x[:, dh:] + mod_ref[5:6, dh:] * acc_hi
        if final_norm:
            ms = (jnp.sum(out_lo * out_lo, axis=-1, keepdims=True)
                  + jnp.sum(out_hi * out_hi, axis=-1, keepdims=True)) / (2 * dh)
            rs = lax.rsqrt(ms + _NORM_EPS)
            out_lo = out_lo * rs * fg_ref[:, :dh]
            out_hi = out_hi * rs * fg_ref[:, dh:]
        o_ref[rows, :dh] = out_lo
        o_ref[rows, dh:] = out_hi
        return carry

    lax.fori_loop(0, tm // 8, rows_body, 0)

    @pl.when(i == n_steps - 1)
    def _():
        wait_slot(1 - slot)


def _combine(ys, dest8, w8t, hp, xa, na_rows, xb, modt, rows_per_mod, s1, s3, s2, final_g, tm):
    n, dh = hp.shape
    d = 2 * dh
    f = s1.shape[1]
    n_steps = n // tm
    na = na_rows // tm
    tpm = rows_per_mod // tm
    n_mod = modt.shape[0]
    smem_spec = lambda fn: pl.BlockSpec((_TOP_K, tm), fn, memory_space=pltpu.SMEM)
    in_specs = [smem_spec(lambda i: (0, i)),
                smem_spec(lambda i: (0, jnp.minimum(i + 1, n_steps - 1))),
                pl.BlockSpec((tm, _TOP_K), lambda i: (i, 0)),
                pl.BlockSpec((tm, dh), lambda i: (i, 0)),
                pl.BlockSpec((tm, d), lambda i: (jnp.minimum(i, na - 1), 0))]
    args = [dest8, dest8, w8t, hp, xa]
    if xb is not None:
        in_specs.append(pl.BlockSpec((tm, d), lambda i: (jnp.maximum(i - na, 0), 0)))
        args.append(xb)
    in_specs += [pl.BlockSpec((None, _N_MOD, d), lambda i: (jnp.minimum(i // tpm, n_mod - 1), 0, 0)),
                 pl.BlockSpec((d, f), lambda i: (0, 0)),
                 pl.BlockSpec((d, f), lambda i: (0, 0)),
                 pl.BlockSpec((f, d), lambda i: (0, 0))]
    args += [modt, s1, s3, s2]
    if final_g is not None:
        in_specs.append(pl.BlockSpec((1, d), lambda i: (0, 0)))
        args.append(final_g)
    in_specs.append(pl.BlockSpec(memory_space=pl.ANY))
    args.append(ys)
    return pl.pallas_call(
        functools.partial(_combine_kernel, na=na, n_steps=n_steps, two_inputs=xb is not None,
                          final_norm=final_g is not None),
        grid=(n_steps,),
        in_specs=in_specs,
        out_specs=pl.BlockSpec((tm, d), lambda i: (i, 0)),
        out_shape=jax.ShapeDtypeStruct((n, d), _F32),
        scratch_shapes=[pltpu.VMEM((2, _TOP_K, tm, dh), _U32), pltpu.SemaphoreType.DMA((2,)),
                        pltpu.VMEM((tm, d), _F32)],
        compiler_params=_params(("arbitrary",), disable_bounds_checks=True),
    )(*args)


def _moe(xa, na_rows, xb, g, modt, rows_per_mod, router_w, router_bias, w1, w3, w2, layer, s1, s3, s2, final_g):
    hp, eidx8, rank8, gate8, counts = _route(xa, na_rows, xb, g, modt, rows_per_mod,
                                             router_w.T, router_bias.reshape(-1, 1), tm=256)
    counts = counts[:, 0].astype(jnp.int32)
    starts = _cumsum_small(counts) - counts
    dest8 = _take_small(starts, eidx8) + rank8
    xs = _dispatch(hp, dest8, tm=256)
    ys = _experts(xs, counts, w1, w3, w2, layer, tmx=256)
    return _combine(ys, dest8, gate8.T, hp, xa, na_rows, xb, modt, rows_per_mod,
                    s1.astype(_BF16), s3.astype(_BF16), s2.astype(_BF16), final_g, tm=128)


def kernel(x, c, ctx, c_ctx, ada_w, ada_b, norm_g, fourier_wo, fourier_bo, da_wqkv, da_wo, da_lambda, da_subln_g, router_w, router_bias, exp_w1, exp_w3, exp_w2, shared_w1, shared_w3, shared_w2, final_g):
    b, s, d = x.shape
    n_ctx = ctx.shape[1]
    depth = ada_w.shape[0]
    assert depth == 2 and b + 1 <= 8 and s % _GRID_W == 0
    nl, nc = b * s, b * n_ctx

    cond8 = jnp.zeros((8, d), _F32).at[:b].set(c).at[b].set(c_ctx)
    mods = _adaln(cond8, ada_w, ada_b).reshape(depth, 8, _N_MOD, d)

    modt = mods[0, :b + 1]
    g_mix, g_moe = norm_g[0, 0:1], norm_g[0, 1:2]
    n2 = 64
    n1 = s // n2
    chan, m1, m2 = _dft_tables(s, n1, n2, d // _FOURIER_GROUPS)
    wo = fourier_wo[0].astype(_BF16)
    bo = fourier_bo[0:1]

    yl = _normmod_chandft(x, g_mix, modt, True, chan, tm=256)
    fl = _seq_dft(yl, m1, m2, n1, n2, tn=256)
    xl = _mm(fl.reshape(nl, d), wo, n_out=d, bias=bo, res=x.reshape(nl, d), gate=modt[:, 2:3], rows_per_gate=s,
             tm=1024, tn=512)

    yc = _normmod_chandft(ctx, g_mix, modt[b:b + 1], False, chan, tm=n_ctx)
    fc = _ctx_seq_dft(yc, tn=512)
    xc = _mm(fc.reshape(nc, d), wo, n_out=d, bias=bo, res=ctx.reshape(nc, d), gate=modt[b:b + 1, 2:3],
             rows_per_gate=nc, tm=n_ctx, tn=512)

    x_all = _moe(xl, nl, xc, g_moe, modt, s, router_w[0], router_bias[0], exp_w1, exp_w3, exp_w2, 0,
                 shared_w1[0], shared_w3[0], shared_w2[0], None)

    modt = mods[1, :b + 1]
    g_mix, g_moe = norm_g[1, 0:1], norm_g[1, 1:2]
    lam_init = 0.8 - 0.6 * float(np.exp(-0.3 * 1))
    wqkv = da_wqkv[0].astype(_BF16)
    cos, sin = _rope_tables(s)

    hl = _normmod(x_all, 0, nl, g_mix, modt, s, 0, tm=512)
    hc = _normmod(x_all, nl, nc, g_mix, modt, nc, b, tm=n_ctx)
    qk, vt = _qkv(hl, wqkv, cos, sin, s, tm=1024, tn=512)
    kvc = _mm(hc, wqkv, n_out=2 * d, w_col0=d, out_dtype=_BF16, tm=n_ctx, tn=512)
    o = _attention(qk.reshape(b, s, 2 * d), vt, kvc.reshape(b, n_ctx, 2 * d), kvc[:, d:].T, da_lambda[0],
                   da_subln_g[0], lam_init, tq=1024, tk=1024)
    xl = _mm(o.reshape(nl, d), da_wo[0].astype(_BF16), n_out=d, res=x_all, gate=modt[:, 2:3], rows_per_gate=s,
             tm=1024, tn=512)

    out = _moe(xl, nl, None, g_moe, modt, s, router_w[1], router_bias[1], exp_w1, exp_w3, exp_w2, 1,
               shared_w1[1], shared_w3[1], shared_w2[1], final_g.reshape(1, d))
    return out.reshape(b, s, d)
```

```python
import functools
import math

import numpy as np
import jax
import jax.numpy as jnp
from jax import lax
from jax.experimental import pallas as pl
from jax.experimental.pallas import tpu as pltpu

_GRID_W = 64
_N_MOD = 6
_NORM_EPS = 1e-6
_FOURIER_GROUPS = 8
_HEAD_DIM = 128
_V_DIM = 2 * _HEAD_DIM
_ROPE_BASE = 10000.0
_TOP_K = 8
_N_GROUPS = 8
_TOPK_GROUPS = 4
_ROUTED_SCALE = 2.5

_LANE = 128
_VMEM_LIMIT = 56 * 1024 * 1024

_F32 = jnp.float32
_BF16 = jnp.bfloat16
_U32 = jnp.uint32
_HI_MASK = np.uint32(0xFFFF0000)
_NEG_INF = float("-inf")


def _params(sem, **kw):
    return pltpu.CompilerParams(dimension_semantics=sem, vmem_limit_bytes=_VMEM_LIMIT, **kw)


def _pack_bf16_pair(lo, hi):
    ul = pltpu.bitcast(lo.astype(_BF16).astype(_F32), _U32)
    uh = pltpu.bitcast(hi.astype(_BF16).astype(_F32), _U32)
    return (ul >> 16) | (uh & _HI_MASK)


def _unpack_lo(u):
    return pltpu.bitcast(u << 16, _F32)


def _unpack_hi(u):
    return pltpu.bitcast(u & _HI_MASK, _F32)


def _rms_mod(x, g, shift, scale):
    ms = jnp.mean(x * x, axis=-1, keepdims=True)
    return x * lax.rsqrt(ms + _NORM_EPS) * g * (1.0 + scale) + shift


def _adaln_kernel(c_ref, w_ref, b_ref, o_ref):
    c = c_ref[...]
    s = (c * jax.nn.sigmoid(c)).astype(_BF16)
    o_ref[...] = jnp.dot(s, w_ref[...].astype(_BF16), preferred_element_type=_F32) + b_ref[...]


def _adaln(cond8, ada_w, ada_b):
    depth, d, n6 = ada_w.shape
    tn = min(512, n6)
    return pl.pallas_call(
        _adaln_kernel,
        grid=(depth, n6 // tn),
        in_specs=[pl.BlockSpec((8, d), lambda l, j: (0, 0)),
                  pl.BlockSpec((None, d, tn), lambda l, j: (l, 0, j)),
                  pl.BlockSpec((None, 1, tn), lambda l, j: (l, 0, j))],
        out_specs=pl.BlockSpec((None, 8, tn), lambda l, j: (l, 0, j)),
        out_shape=jax.ShapeDtypeStruct((depth, 8, n6), _F32),
        compiler_params=_params(("arbitrary", "arbitrary")),
    )(cond8, ada_w, ada_b.reshape(depth, 1, n6))


def _normmod_chandft_kernel(x_ref, g_ref, mod_ref, tab_ref, o_ref, *, groups):
    h = _rms_mod(x_ref[...], g_ref[...], mod_ref[0:1, :], mod_ref[1:2, :]).astype(_BF16)
    cg = h.shape[-1] // groups
    for g in range(groups):
        r = jnp.dot(h[:, g * cg:(g + 1) * cg], tab_ref[...], preferred_element_type=_F32)
        o_ref[:, g * cg:(g + 1) * cg] = _pack_bf16_pair(r[:, :cg], r[:, cg:])


def _normmod_chandft(x3, g, mod, mod_per_batch, tab, tm):
    b, t, d = x3.shape
    mod_map = (lambda bi, i: (bi, 0, 0)) if mod_per_batch else (lambda bi, i: (0, 0, 0))
    return pl.pallas_call(
        functools.partial(_normmod_chandft_kernel, groups=_FOURIER_GROUPS),
        grid=(b, t // tm),
        in_specs=[pl.BlockSpec((None, tm, d), lambda bi, i: (bi, i, 0)),
                  pl.BlockSpec((1, d), lambda bi, i: (0, 0)),
                  pl.BlockSpec((None, _N_MOD, d), mod_map),
                  pl.BlockSpec(tab.shape, lambda bi, i: (0, 0))],
        out_specs=pl.BlockSpec((None, tm, d), lambda bi, i: (bi, i, 0)),
        out_shape=jax.ShapeDtypeStruct((b, t, d), _U32),
        compiler_params=_params(("arbitrary", "arbitrary")),
    )(x3, g, mod, tab)


def _normmod_kernel(x_ref, g_ref, mod_ref, o_ref):
    o_ref[...] = _rms_mod(x_ref[...], g_ref[...], mod_ref[0:1, :], mod_ref[1:2, :]).astype(_BF16)


def _normmod(x2, row0, rows, g, mod, rows_per_mod, mod0, tm):
    d = x2.shape[1]
    t0 = row0 // tm
    tpm = rows_per_mod // tm
    return pl.pallas_call(
        _normmod_kernel,
        grid=(rows // tm,),
        in_specs=[pl.BlockSpec((tm, d), lambda i: (t0 + i, 0)),
                  pl.BlockSpec((1, d), lambda i: (0, 0)),
                  pl.BlockSpec((None, _N_MOD, d), lambda i: (mod0 + i // tpm, 0, 0))],
        out_specs=pl.BlockSpec((tm, d), lambda i: (i, 0)),
        out_shape=jax.ShapeDtypeStruct((rows, d), _BF16),
        compiler_params=_params(("arbitrary",)),
    )(x2, g, mod)


def _unpack_complex_rows(u):
    return jnp.concatenate([_unpack_lo(u), _unpack_hi(u)], axis=0).astype(_BF16)


def _seq_dft_kernel(*refs, n1, n2, n_col):
    y_refs = refs[:n_col]
    m1_ref, m2_ref, o_ref, a_sc, f_sc = refs[n_col:]

    def stage1(t1, carry):
        u = jnp.concatenate([y[pl.ds(t1, n2, stride=n1), :] for y in y_refs], axis=1)
        r = jnp.dot(m1_ref[t1], _unpack_complex_rows(u), preferred_element_type=_F32)
        packed = _pack_bf16_pair(r[:n2], r[n2:])
        for c in range(n_col):
            a_sc[c, pl.ds(t1, n2, stride=n1), :] = packed[:, c * _LANE:(c + 1) * _LANE]
        return carry

    lax.fori_loop(0, n1, stage1, 0, unroll=4)

    def stage2(k2, carry):
        rows = pl.ds(pl.multiple_of(k2 * n1, n1), n1)
        u = jnp.concatenate([a_sc[c, rows, :] for c in range(n_col)], axis=1)
        r = jnp.dot(m2_ref[...], _unpack_complex_rows(u), preferred_element_type=_F32)
        for c in range(n_col):
            f_sc[c, pl.ds(k2, n1, stride=n2), :] = r[:, c * _LANE:(c + 1) * _LANE]
        return carry

    lax.fori_loop(0, n2, stage2, 0, unroll=4)
    for c in range(n_col):
        o_ref[:, c * _LANE:(c + 1) * _LANE] = f_sc[c].astype(_BF16)


def _cos_sin(n, rows, cols):
    ang = 2.0 * np.pi * ((np.arange(rows, dtype=np.int64)[:, None] * np.arange(cols, dtype=np.int64)[None, :]) % n) / n
    return np.cos(ang), np.sin(ang)


def _dft_tables(t, n1, n2, cg):
    cc, sc = _cos_sin(cg, cg, cg)
    chan = np.concatenate([cc, -sc], axis=1) / math.sqrt(cg)
    k2 = np.arange(n2, dtype=np.int64)
    idx = (k2[None, :, None] * k2[None, None, :] * n1 + k2[None, :, None] * np.arange(n1, dtype=np.int64)[:, None, None]) % t
    gr = np.cos(2.0 * np.pi * idx / t) / math.sqrt(t)
    gi = -np.sin(2.0 * np.pi * idx / t) / math.sqrt(t)
    m1 = np.concatenate([np.concatenate([gr, -gi], axis=2), np.concatenate([gi, gr], axis=2)], axis=1)
    c1, s1 = _cos_sin(n1, n1, n1)
    m2 = np.concatenate([c1, s1], axis=1)
    return jnp.asarray(chan, _BF16), jnp.asarray(m1, _BF16), jnp.asarray(m2, _BF16)


def _seq_dft(y, m1, m2, n1, n2, tn):
    b, t, d = y.shape
    n_col = tn // _LANE
    col_spec = lambda c: pl.BlockSpec((None, t, _LANE), lambda bi, j: (bi, 0, n_col * j + c))
    return pl.pallas_call(
        functools.partial(_seq_dft_kernel, n1=n1, n2=n2, n_col=n_col),
        grid=(b, d // tn),
        in_specs=[col_spec(c) for c in range(n_col)]
        + [pl.BlockSpec(m1.shape, lambda bi, j: (0, 0, 0)),
           pl.BlockSpec(m2.shape, lambda bi, j: (0, 0))],
        out_specs=pl.BlockSpec((None, t, tn), lambda bi, j: (bi, 0, j)),
        out_shape=jax.ShapeDtypeStruct((b, t, d), _BF16),
        scratch_shapes=[pltpu.VMEM((n_col, t, _LANE), _U32), pltpu.VMEM((n_col, t, _LANE), _F32)],
        compiler_params=_params(("arbitrary", "arbitrary")),
    )(*([y] * n_col), m1, m2)


def _ctx_dft_kernel(tab_ref, y_ref, o_ref):
    o_ref[...] = jnp.dot(tab_ref[...], _unpack_complex_rows(y_ref[...]), preferred_element_type=_F32).astype(_BF16)


def _ctx_seq_dft(y, tn):
    b, t, d = y.shape
    c, s = _cos_sin(t, t, t)
    tab = jnp.asarray(np.concatenate([c, s], axis=1) / math.sqrt(t), _BF16)
    return pl.pallas_call(
        _ctx_dft_kernel,
        grid=(b, d // tn),
        in_specs=[pl.BlockSpec(tab.shape, lambda bi, j: (0, 0)),
                  pl.BlockSpec((None, t, tn), lambda bi, j: (bi, 0, j))],
        out_specs=pl.BlockSpec((None, t, tn), lambda bi, j: (bi, 0, j)),
        out_shape=jax.ShapeDtypeStruct((b, t, d), _BF16),
        compiler_params=_params(("arbitrary", "arbitrary")),
    )(tab, y)


def _mm_kernel(*refs, has_bias, has_res):
    a_ref, w_ref = refs[0], refs[1]
    o_ref = refs[-1]
    acc = jnp.dot(a_ref[...], w_ref[...], preferred_element_type=_F32)
    k = 2
    if has_bias:
        acc = acc + refs[k][...]
        k += 1
    if has_res:
        acc = refs[k][...] + refs[k + 1][...] * acc
    o_ref[...] = acc.astype(o_ref.dtype)


def _mm(a, w, *, n_out, w_col0=0, bias=None, res=None, res_row0=0, gate=None, rows_per_gate=None, gate0=0,
        out_dtype=_F32, tm, tn):
    m, k = a.shape
    wj0 = w_col0 // tn
    in_specs = [pl.BlockSpec((tm, k), lambda i, j: (i, 0)),
                pl.BlockSpec((k, tn), lambda i, j: (0, wj0 + j))]
    args = [a, w]
    if bias is not None:
        in_specs.append(pl.BlockSpec((1, tn), lambda i, j: (0, j)))
        args.append(bias)
    if res is not None:
        r0 = res_row0 // tm
        tpg = rows_per_gate // tm
        in_specs.append(pl.BlockSpec((tm, tn), lambda i, j: (r0 + i, j)))
        in_specs.append(pl.BlockSpec((None, 1, tn), lambda i, j: (gate0 + i // tpg, 0, j)))
        args += [res, gate]
    return pl.pallas_call(
        functools.partial(_mm_kernel, has_bias=bias is not None, has_res=res is not None),
        grid=(m // tm, n_out // tn),
        in_specs=in_specs,
        out_specs=pl.BlockSpec((tm, tn), lambda i, j: (i, j)),
        out_shape=jax.ShapeDtypeStruct((m, n_out), out_dtype),
        compiler_params=_params(("arbitrary", "arbitrary")),
    )(*args)


def _qkv_kernel(a_ref, w_ref, cos_ref, sin_ref, o_ref, vt_ref, *, nq, nk, scale):
    j = pl.program_id(1)
    acc = jnp.dot(a_ref[...], w_ref[...], preferred_element_type=_F32)
    tn = acc.shape[1]

    @pl.when(j >= nq + nk)
    def _():
        vt_ref[...] = acc.T.astype(_BF16)

    @pl.when(j < nq + nk)
    def _():
        cos = cos_ref[...]
        sin = sin_ref[...]
        sc = jnp.where(j < nq, scale, 1.0).astype(_F32)
        for u in range(tn // _HEAD_DIM):
            x = acc[:, u * _HEAD_DIM:(u + 1) * _HEAD_DIM]
            sw = pltpu.roll(x, _HEAD_DIM // 2, 1)
            o_ref[:, u * _HEAD_DIM:(u + 1) * _HEAD_DIM] = ((x * cos + sw * sin) * sc).astype(_BF16)


def _rope_tables(s):
    n_freq = _HEAD_DIM // 4
    freqs = _ROPE_BASE ** (-jnp.arange(n_freq, dtype=_F32) / n_freq)
    pos = jnp.arange(s, dtype=jnp.int32)
    ang_r = (pos // _GRID_W).astype(_F32)[:, None] * freqs[None, :]
    ang_c = (pos % _GRID_W).astype(_F32)[:, None] * freqs[None, :]
    cos = jnp.concatenate([jnp.cos(ang_r), jnp.cos(ang_c)] * 2, axis=1)
    sin = jnp.concatenate([-jnp.sin(ang_r), -jnp.sin(ang_c), jnp.sin(ang_r), jnp.sin(ang_c)], axis=1)
    return cos, sin


def _rope_friendly_qkv_weights(wqkv):
    d = wqkv.shape[0]
    q4 = _HEAD_DIM // 4
    wqk = wqkv[:, :2 * d].reshape(d, 2 * d // _HEAD_DIM, 2, 2, q4)
    wqk = jnp.swapaxes(wqk, 2, 3).reshape(d, 2 * d)
    return jnp.concatenate([wqk, wqkv[:, 2 * d:]], axis=1).astype(_BF16)


def _qkv(a, w, cos, sin, s, tm, tn):
    m, k = a.shape
    d = w.shape[1] // 3
    tps = s // tm
    nqk = 2 * d // tn
    return pl.pallas_call(
        functools.partial(_qkv_kernel, nq=d // tn, nk=d // tn, scale=_HEAD_DIM ** -0.5 * math.log2(math.e)),
        grid=(m // tm, 3 * d // tn),
        in_specs=[pl.BlockSpec((tm, k), lambda i, j: (i, 0)),
                  pl.BlockSpec((k, tn), lambda i, j: (0, j)),
                  pl.BlockSpec((tm, _HEAD_DIM), lambda i, j: (i % tps, 0)),
                  pl.BlockSpec((tm, _HEAD_DIM), lambda i, j: (i % tps, 0))],
        out_specs=[pl.BlockSpec((tm, tn), lambda i, j: (i, jnp.minimum(j, nqk - 1))),
                   pl.BlockSpec((tn, tm), lambda i, j: (jnp.maximum(j - nqk, 0), i))],
        out_shape=[jax.ShapeDtypeStruct((m, 2 * d), _BF16), jax.ShapeDtypeStruct((d, m), _BF16)],
        compiler_params=_params(("arbitrary", "arbitrary")),
    )(a, w, cos, sin)


def _attn_kernel(q_ref, k_ref, vt_ref, kc_ref, vct_ref, lam_ref, g_ref, o_ref,
                 s0, s1, p0, p1, m0, m1, a0, a1, x0, x1, acc0, acc1, *, tk, n_chunks, lam_init):
    s_sc, p_sc, m_sc, a_sc, x_sc, acc_sc = (s0, s1), (p0, p1), (m0, m1), (a0, a1), (x0, x1), (acc0, acc1)
    q = q_ref[...]
    qs = (q[:, :_HEAD_DIM], q[:, _HEAD_DIM:])
    dn = (((1,), (1,)), ((), ()))

    def scores(kb, c):
        return lax.dot_general(kb[:, c * _HEAD_DIM:(c + 1) * _HEAD_DIM], qs[c], dn, preferred_element_type=_F32)

    def scores_to_scratch(kb, c):
        st = scores(kb, c)
        s_sc[c][...] = st
        x_sc[c][...] = jnp.max(st, axis=0, keepdims=True)

    def chunk(i):
        return pl.ds(i * tk, tk) if isinstance(i, int) else pl.ds(pl.multiple_of(i * tk, tk), tk)

    def k_chunk(i):
        return k_ref[chunk(i), :]

    def vt_chunk(i):
        return vt_ref[:, chunk(i)]

    dv = vt_ref.shape[0]

    def with_ones(vtb):
        return jnp.concatenate([vtb, jnp.ones((16, vtb.shape[1]), _BF16)], axis=0)

    kcb = kc_ref[...]
    vctb = with_ones(vct_ref[...])
    st_ctx = [scores(kcb, c) for c in range(2)]
    scores_to_scratch(k_chunk(0), 0)
    for c in range(2):
        m = jnp.max(st_ctx[c], axis=0, keepdims=True)
        p = jnp.exp2(st_ctx[c] - m)
        m_sc[c][...] = m
        acc_sc[c][...] = jnp.dot(vctb, p.astype(_BF16), preferred_element_type=_F32)
        a_sc[c][...] = jnp.ones(a_sc[c].shape, _F32)
    p_sc[1][...] = jnp.zeros(p_sc[1].shape, _BF16)

    def softmax(c):
        m_old = m_sc[c][...]
        m_new = jnp.maximum(m_old, x_sc[c][...])
        a_sc[c][...] = jnp.exp2(m_old - m_new)
        m_sc[c][...] = m_new
        p_sc[c][...] = jnp.exp2(s_sc[c][...] - m_new).astype(_BF16)

    def pv(c, i):
        acc_sc[c][...] = (a_sc[c][...] * acc_sc[c][...]
                          + jnp.dot(with_ones(vt_chunk(i)), p_sc[c][...], preferred_element_type=_F32))

    def body(i, carry):
        scores_to_scratch(k_chunk(i), 1)
        softmax(0)
        pv(1, jnp.maximum(i - 1, 0))
        scores_to_scratch(k_chunk(i + 1), 0)
        softmax(1)
        pv(0, i)
        return carry

    lax.fori_loop(0, n_chunks - 1, body, 0)
    last = n_chunks - 1
    scores_to_scratch(k_chunk(last), 1)
    softmax(0)
    pv(1, max(last - 1, 0))
    softmax(1)
    pv(0, last)
    pv(1, last)

    lp = lam_ref[...]
    lam = (jnp.exp(jnp.sum(lp[0:1] * lp[1:2], axis=-1, keepdims=True))
           - jnp.exp(jnp.sum(lp[2:3] * lp[3:4], axis=-1, keepdims=True)) + lam_init)
    ot = acc0[:dv, :] * (1.0 / acc0[dv:dv + 1, :]) - acc1[:dv, :] * (lam / acc1[dv:dv + 1, :])
    ms = jnp.mean(ot * ot, axis=0, keepdims=True)
    ot = ot * lax.rsqrt(ms + _NORM_EPS) * g_ref[...] * (1.0 - lam_init)
    o_ref[...] = ot.T.astype(_BF16)


def _attention(qk, vt, kvc, vct, lam_p, subln_g, lam_init, tq, tk):
    b, s, d2 = qk.shape
    d = d2 // 2
    h = d // _V_DIM
    c = kvc.shape[1]
    return pl.pallas_call(
        functools.partial(_attn_kernel, tk=tk, n_chunks=s // tk, lam_init=lam_init),
        grid=(b, h, s // tq),
        in_specs=[pl.BlockSpec((None, tq, _V_DIM), lambda bi, hi, qi: (bi, qi, hi)),
                  pl.BlockSpec((None, s, _V_DIM), lambda bi, hi, qi: (bi, 0, h + hi), pipeline_mode=pl.Buffered(1)),
                  pl.BlockSpec((_V_DIM, s), lambda bi, hi, qi: (hi, bi), pipeline_mode=pl.Buffered(1)),
                  pl.BlockSpec((None, c, _V_DIM), lambda bi, hi, qi: (bi, 0, hi)),
                  pl.BlockSpec((_V_DIM, c), lambda bi, hi, qi: (hi, bi)),
                  pl.BlockSpec(lam_p.shape, lambda bi, hi, qi: (0, 0)),
                  pl.BlockSpec((_V_DIM, 1), lambda bi, hi, qi: (0, 0))],
        out_specs=pl.BlockSpec((None, tq, _V_DIM), lambda bi, hi, qi: (bi, qi, hi)),
        out_shape=jax.ShapeDtypeStruct((b, s, d), _BF16),
        scratch_shapes=([pltpu.VMEM((tk, tq), _F32)] * 2 + [pltpu.VMEM((tk, tq), _BF16)] * 2
                        + [pltpu.VMEM((1, tq), _F32)] * 6 + [pltpu.VMEM((_V_DIM + 16, tq), _F32)] * 2),
        compiler_params=_params(("arbitrary", "arbitrary", "arbitrary")),
    )(qk, qk, vt, kvc, vct, lam_p, subln_g.reshape(_V_DIM, 1))


def _route_kernel(*refs, na, two_inputs, n_exp):
    if two_inputs:
        xa_ref, xb_ref = refs[0], refs[1]
        refs = refs[2:]
    else:
        xa_ref, xb_ref = refs[0], None
        refs = refs[1:]
    g_ref, mod_ref, wt_ref, bias_ref, hp_ref, eidx_ref, rank_ref, gate_ref, cnt_ref, carry_sc = refs
    i = pl.program_id(0)

    @pl.when(i == 0)
    def _():
        carry_sc[...] = jnp.zeros(carry_sc.shape, _F32)

    x = xa_ref[...]
    if two_inputs:
        x = jnp.where(i < na, x, xb_ref[...])
    tm, d = x.shape
    h = _rms_mod(x, g_ref[...], mod_ref[3:4, :], mod_ref[4:5, :])
    hp_ref[...] = _pack_bf16_pair(h[:, :d // 2], h[:, d // 2:])

    h_hi = h.astype(_BF16)
    h_lo = (h - h_hi.astype(_F32)).astype(_BF16)
    w = wt_ref[...]
    w_hi = w.astype(_BF16)
    w_lo = (w - w_hi.astype(_F32)).astype(_BF16)
    dn = (((1,), (1,)), ((), ()))
    logits = (lax.dot_general(w_hi, h_hi, dn, preferred_element_type=_F32)
              + lax.dot_general(w_lo, h_hi, dn, preferred_element_type=_F32)
              + lax.dot_general(w_hi, h_lo, dn, preferred_element_type=_F32))
    scores = jax.nn.sigmoid(logits)
    sel = scores + bias_ref[...]

    pg = n_exp // _N_GROUPS
    shp = (_N_GROUPS, pg, tm)
    sel3 = sel.reshape(shp)
    scores3 = scores.reshape(shp)
    midx = lax.broadcasted_iota(jnp.int32, shp, 1).astype(_F32)
    gidx = lax.broadcasted_iota(jnp.int32, (_N_GROUPS, 1, tm), 0).astype(_F32)
    eid3 = lax.broadcasted_iota(jnp.int32, shp, 0).astype(_F32) * pg + midx

    m1 = jnp.max(sel3, axis=1, keepdims=True)
    f1 = jnp.min(jnp.where(sel3 == m1, midx, float(pg)), axis=1, keepdims=True)
    m2 = jnp.max(jnp.where(midx == f1, _NEG_INF, sel3), axis=1, keepdims=True)
    grp = m1 + m2

    gsel = jnp.zeros((_N_GROUPS, 1, tm), _F32)
    for _ in range(_TOPK_GROUPS):
        m = jnp.max(grp, axis=0, keepdims=True)
        f = jnp.min(jnp.where(grp == m, gidx, float(_N_GROUPS)), axis=0, keepdims=True)
        hit = gidx == f
        gsel = jnp.where(hit, 1.0, gsel)
        grp = jnp.where(hit, _NEG_INF, grp)
    cur = jnp.where(gsel > 0.0, sel3, _NEG_INF)

    chosen = jnp.zeros(shp, _F32)
    firsts = []
    for _ in range(_TOP_K):
        m = jnp.max(jnp.max(cur, axis=1, keepdims=True), axis=0, keepdims=True)
        f = jnp.min(jnp.min(jnp.where(cur == m, eid3, float(n_exp)), axis=1, keepdims=True), axis=0, keepdims=True)
        hit = eid3 == f
        chosen = jnp.where(hit, 1.0, chosen)
        cur = jnp.where(hit, _NEG_INF, cur)
        firsts.append(f)

    wsel = scores3 * chosen
    wsum = jnp.sum(jnp.sum(wsel, axis=1, keepdims=True), axis=0, keepdims=True)
    gates3 = wsel / wsum * _ROUTED_SCALE

    ch2 = chosen.reshape(n_exp, tm)
    upper = (lax.broadcasted_iota(jnp.int32, (tm, tm), 0) < lax.broadcasted_iota(jnp.int32, (tm, tm), 1))
    prefix = jnp.dot(ch2.astype(_BF16), jnp.where(upper, 1.0, 0.0).astype(_BF16), preferred_element_type=_F32)
    carry = carry_sc[...]
    rank3 = (carry + prefix).reshape(shp)
    new_carry = carry + jnp.sum(ch2, axis=-1, keepdims=True)
    carry_sc[...] = new_carry
    cnt_ref[...] = new_carry

    for r in range(_TOP_K):
        hit = eid3 == firsts[r]
        rk = jnp.sum(jnp.sum(jnp.where(hit, rank3, 0.0), axis=1, keepdims=True), axis=0, keepdims=True)
        gt = jnp.sum(jnp.sum(jnp.where(hit, gates3, 0.0), axis=1, keepdims=True), axis=0, keepdims=True)
        eidx_ref[r:r + 1, :] = firsts[r].reshape(1, tm).astype(jnp.int32)
        rank_ref[r:r + 1, :] = rk.reshape(1, tm).astype(jnp.int32)
        gate_ref[r:r + 1, :] = gt.reshape(1, tm)


def _route(xa, na_rows, xb, g, modt, rows_per_mod, wt, bias, tm):
    d = xa.shape[1]
    n_exp = wt.shape[0]
    na = na_rows // tm
    nb = 0 if xb is None else xb.shape[0] // tm
    n = (na + nb) * tm
    tpm = rows_per_mod // tm
    n_mod = modt.shape[0]
    in_specs = [pl.BlockSpec((tm, d), lambda i: (jnp.minimum(i, na - 1), 0))]
    args = [xa]
    if xb is not None:
        in_specs.append(pl.BlockSpec((tm, d), lambda i: (jnp.maximum(i - na, 0), 0)))
        args.append(xb)
    in_specs += [pl.BlockSpec((1, d), lambda i: (0, 0)),
                 pl.BlockSpec((None, _N_MOD, d), lambda i: (jnp.minimum(i // tpm, n_mod - 1), 0, 0)),
                 pl.BlockSpec((n_exp, d), lambda i: (0, 0)),
                 pl.BlockSpec((n_exp, 1), lambda i: (0, 0))]
    args += [g, modt, wt, bias]
    return pl.pallas_call(
        functools.partial(_route_kernel, na=na, two_inputs=xb is not None, n_exp=n_exp),
        grid=(na + nb,),
        in_specs=in_specs,
        out_specs=[pl.BlockSpec((tm, d // 2), lambda i: (i, 0)),
                   pl.BlockSpec((_TOP_K, tm), lambda i: (0, i)),
                   pl.BlockSpec((_TOP_K, tm), lambda i: (0, i)),
                   pl.BlockSpec((_TOP_K, tm), lambda i: (0, i)),
                   pl.BlockSpec((n_exp, 1), lambda i: (0, 0))],
        out_shape=[jax.ShapeDtypeStruct((n, d // 2), _U32),
                   jax.ShapeDtypeStruct((_TOP_K, n), jnp.int32),
                   jax.ShapeDtypeStruct((_TOP_K, n), jnp.int32),
                   jax.ShapeDtypeStruct((_TOP_K, n), _F32),
                   jax.ShapeDtypeStruct((n_exp, 1), _F32)],
        scratch_shapes=[pltpu.VMEM((n_exp, 1), _F32)],
        compiler_params=_params(("arbitrary",)),
    )(*args)


def _dispatch_kernel(dest_ref, hp_ref, xs_ref, sem):
    tm = hp_ref.shape[0]

    def row_copy(n, k):
        return pltpu.make_async_copy(hp_ref.at[pl.ds(n, 1), :], xs_ref.at[pl.ds(dest_ref[k, n], 1), :], sem)

    def body(n, carry):
        for k in range(_TOP_K):
            row_copy(n, k).start()
        return carry

    lax.fori_loop(0, tm, body, 0, unroll=4)
    for _ in range(_TOP_K):
        pltpu.make_async_copy(hp_ref, xs_ref.at[pl.ds(0, tm), :], sem).wait()


def _dispatch(hp, dest8, tm):
    n, dh = hp.shape
    return pl.pallas_call(
        _dispatch_kernel,
        grid=(n // tm,),
        in_specs=[pl.BlockSpec((_TOP_K, tm), lambda i: (0, i), memory_space=pltpu.SMEM),
                  pl.BlockSpec((tm, dh), lambda i: (i, 0))],
        out_specs=pl.BlockSpec(memory_space=pl.ANY),
        out_shape=jax.ShapeDtypeStruct((n * _TOP_K, dh), _U32),
        scratch_shapes=[pltpu.SemaphoreType.DMA(())],
        compiler_params=_params(("arbitrary",), disable_bounds_checks=True),
    )(dest8, hp)


def _swiglu_packed(xu, w1, w3, w2):
    dh = xu.shape[1]
    x_lo = _unpack_lo(xu).astype(_BF16)
    x_hi = _unpack_hi(xu).astype(_BF16)
    h1 = (jnp.dot(x_lo, w1[:dh], preferred_element_type=_F32) + jnp.dot(x_hi, w1[dh:], preferred_element_type=_F32))
    h3 = (jnp.dot(x_lo, w3[:dh], preferred_element_type=_F32) + jnp.dot(x_hi, w3[dh:], preferred_element_type=_F32))
    a = (h1 * jax.nn.sigmoid(h1) * h3).astype(_BF16)
    return jnp.dot(a, w2, preferred_element_type=_F32)


def _expert_kernel(tile_ref, exp_ref, lo_ref, hi_ref, first_ref, newexp_ref, valid_ref,
                   x_ref, w1_ref, w3_ref, w2_ref, y_ref, w1b, w3b, w2b):
    v = pl.program_id(0)

    @pl.when(newexp_ref[v] == 1)
    def _():
        w1b[...] = w1_ref[...].astype(_BF16)
        w3b[...] = w3_ref[...].astype(_BF16)
        w2b[...] = w2_ref[...].astype(_BF16)

    @pl.when(valid_ref[v] == 1)
    def _():
        tmx, dh = x_ref.shape
        y = _swiglu_packed(x_ref[...], w1b[...], w3b[...], w2b[...])
        packed = _pack_bf16_pair(y[:, :dh], y[:, dh:])
        rows = lax.broadcasted_iota(jnp.int32, (tmx, 1), 0)
        mine = (rows >= lo_ref[v]) & (rows < hi_ref[v])

        @pl.when(first_ref[v] == 1)
        def _():
            y_ref[...] = jnp.where(mine, packed, jnp.zeros_like(packed))

        @pl.when(first_ref[v] == 0)
        def _():
            y_ref[...] = jnp.where(mine, packed, y_ref[...])


def _cumsum_small(v):
    n = v.shape[0]
    tri = jnp.arange(n)[:, None] >= jnp.arange(n)[None, :]
    return jnp.sum(jnp.where(tri, v[None, :], 0), axis=1).astype(jnp.int32)


def _take_small(table, idx):
    hot = idx[..., None] == jnp.arange(table.shape[0], dtype=jnp.int32)
    return jnp.sum(jnp.where(hot, table, 0), axis=-1).astype(jnp.int32)


def _expert_visits(counts, n_rows, tmx):
    n_exp = counts.shape[0]
    n_tiles = n_rows // tmx
    n_vis = n_tiles + n_exp - 1
    ends = _cumsum_small(counts)
    starts = ends - counts
    t_first = starts // tmx
    t_last = jnp.where(counts > 0, (ends - 1) // tmx, t_first - 1)
    nv = t_last - t_first + 1
    v_end = _cumsum_small(nv)
    v_start = v_end - nv
    total = v_end[-1]
    v = jnp.arange(n_vis, dtype=jnp.int32)
    vc = jnp.minimum(v, total - 1)
    e = jnp.sum((v_end[None, :] <= vc[:, None]).astype(jnp.int32), axis=1)
    tile = _take_small(t_first, e) + (vc - _take_small(v_start, e))
    lo = jnp.clip(_take_small(starts, e) - tile * tmx, 0, tmx)
    hi = jnp.clip(_take_small(ends, e) - tile * tmx, 0, tmx)
    valid = (v < total).astype(jnp.int32)
    prev_tile = jnp.concatenate([jnp.full((1,), -1, jnp.int32), tile[:-1]])
    prev_e = jnp.concatenate([jnp.full((1,), -1, jnp.int32), e[:-1]])
    first = ((tile != prev_tile) & (valid == 1)).astype(jnp.int32)
    newexp = ((e != prev_e) & (valid == 1)).astype(jnp.int32)
    return tile, e, lo, hi, first, newexp, valid


def _experts(xs, counts, w1, w3, w2, layer, tmx):
    p, dh = xs.shape
    _, n_exp, d, f = w1.shape
    meta = _expert_visits(counts, p, tmx)
    n_vis = meta[0].shape[0]
    grid_spec = pltpu.PrefetchScalarGridSpec(
        num_scalar_prefetch=7,
        grid=(n_vis,),
        in_specs=[pl.BlockSpec((tmx, dh), lambda v, t, e, *_: (t[v], 0)),
                  pl.BlockSpec((None, None, d, f), lambda v, t, e, *_: (layer, e[v], 0, 0)),
                  pl.BlockSpec((None, None, d, f), lambda v, t, e, *_: (layer, e[v], 0, 0)),
                  pl.BlockSpec((None, None, f, d), lambda v, t, e, *_: (layer, e[v], 0, 0))],
        out_specs=pl.BlockSpec((tmx, dh), lambda v, t, e, *_: (t[v], 0)),
        scratch_shapes=[pltpu.VMEM((d, f), _BF16), pltpu.VMEM((d, f), _BF16), pltpu.VMEM((f, d), _BF16)],
    )
    return pl.pallas_call(
        _expert_kernel,
        grid_spec=grid_spec,
        out_shape=jax.ShapeDtypeStruct((p, dh), _U32),
        compiler_params=_params(("arbitrary",)),
    )(*meta, xs, w1, w3, w2)


def _combine_kernel(*refs, na, n_steps, two_inputs, final_norm):
    dcur_ref, dnext_ref, w_ref, hp_ref, xa_ref = refs[:5]
    refs = refs[5:]
    xb_ref = None
    if two_inputs:
        xb_ref, refs = refs[0], refs[1:]
    mod_ref, s1_ref, s3_ref, s2_ref = refs[:4]
    refs = refs[4:]
    fg_ref = None
    if final_norm:
        fg_ref, refs = refs[0], refs[1:]
    ys_ref, o_ref, gbuf, sems, ysh_sc = refs
    i = pl.program_id(0)
    slot = i % 2
    tm, dh = hp_ref.shape

    def row_copy(dref, sl, n, k):
        return pltpu.make_async_copy(ys_ref.at[pl.ds(dref[k, n], 1), :], gbuf.at[sl, k, pl.ds(n, 1), :], sems.at[sl])

    def issue(dref, sl):
        def body(n, carry):
            for k in range(_TOP_K):
                row_copy(dref, sl, n, k).start()
            return carry

        lax.fori_loop(0, tm, body, 0)

    def wait_slot(sl):
        for k in range(_TOP_K):
            pltpu.make_async_copy(ys_ref.at[pl.ds(0, tm), :], gbuf.at[sl, k], sems.at[sl]).wait()

    @pl.when(i == 0)
    def _():
        issue(dcur_ref, 0)

    ysh_sc[...] = _swiglu_packed(hp_ref[...], s1_ref[...], s3_ref[...], s2_ref[...])
    wait_slot(slot)

    def rows_body(r, carry):
        for t in range(8):
            for k in range(_TOP_K):
                row_copy(dnext_ref, 1 - slot, r * 8 + t, k).start()
        rows = pl.ds(pl.multiple_of(r * 8, 8), 8)
        acc_lo = ysh_sc[rows, :dh]
        acc_hi = ysh_sc[rows, dh:]
        for k in range(_TOP_K):
            u = gbuf[slot, k, rows, :]
            wk = w_ref[rows, k:k + 1]
            acc_lo = acc_lo + wk * _unpack_lo(u)
            acc_hi = acc_hi + wk * _unpack_hi(u)
        x = xa_ref[rows, :]
        if two_inputs:
            x = jnp.where(i < na, x, xb_ref[rows, :])
        out_lo = x[:, :dh] + mod_ref[5:6, :dh] * acc_lo
        out_hi = x[:, dh:] + mod_ref[5:6, dh:] * acc_hi
        if final_norm:
            ms = (jnp.sum(out_lo * out_lo, axis=-1, keepdims=True)
                  + jnp.sum(out_hi * out_hi, axis=-1, keepdims=True)) / (2 * dh)
            rs = lax.rsqrt(ms + _NORM_EPS)
            out_lo = out_lo * rs * fg_ref[:, :dh]
            out_hi = out_hi * rs * fg_ref[:, dh:]
        o_ref[rows, :dh] = out_lo
        o_ref[rows, dh:] = out_hi
        return carry

    lax.fori_loop(0, tm // 8, rows_body, 0)

    @pl.when(i == n_steps - 1)
    def _():
        wait_slot(1 - slot)


def _combine(ys, dest8, w8t, hp, xa, na_rows, xb, modt, rows_per_mod, s1, s3, s2, final_g, tm):
    n, dh = hp.shape
    d = 2 * dh
    f = s1.shape[1]
    n_steps = n // tm
    na = na_rows // tm
    tpm = rows_per_mod // tm
    n_mod = modt.shape[0]
    smem_spec = lambda fn: pl.BlockSpec((_TOP_K, tm), fn, memory_space=pltpu.SMEM)
    in_specs = [smem_spec(lambda i: (0, i)),
                smem_spec(lambda i: (0, jnp.minimum(i + 1, n_steps - 1))),
                pl.BlockSpec((tm, _TOP_K), lambda i: (i, 0)),
                pl.BlockSpec((tm, dh), lambda i: (i, 0)),
                pl.BlockSpec((tm, d), lambda i: (jnp.minimum(i, na - 1), 0))]
    args = [dest8, dest8, w8t, hp, xa]
    if xb is not None:
        in_specs.append(pl.BlockSpec((tm, d), lambda i: (jnp.maximum(i - na, 0), 0)))
        args.append(xb)
    in_specs += [pl.BlockSpec((None, _N_MOD, d), lambda i: (jnp.minimum(i // tpm, n_mod - 1), 0, 0)),
                 pl.BlockSpec((d, f), lambda i: (0, 0)),
                 pl.BlockSpec((d, f), lambda i: (0, 0)),
                 pl.BlockSpec((f, d), lambda i: (0, 0))]
    args += [modt, s1, s3, s2]
    if final_g is not None:
        in_specs.append(pl.BlockSpec((1, d), lambda i: (0, 0)))
        args.append(final_g)
    in_specs.append(pl.BlockSpec(memory_space=pl.ANY))
    args.append(ys)
    return pl.pallas_call(
        functools.partial(_combine_kernel, na=na, n_steps=n_steps, two_inputs=xb is not None,
                          final_norm=final_g is not None),
        grid=(n_steps,),
        in_specs=in_specs,
        out_specs=pl.BlockSpec((tm, d), lambda i: (i, 0)),
        out_shape=jax.ShapeDtypeStruct((n, d), _F32),
        scratch_shapes=[pltpu.VMEM((2, _TOP_K, tm, dh), _U32), pltpu.SemaphoreType.DMA((2,)),
                        pltpu.VMEM((tm, d), _F32)],
        compiler_params=_params(("arbitrary",), disable_bounds_checks=True),
    )(*args)


def _moe(xa, na_rows, xb, g, modt, rows_per_mod, router_w, router_bias, w1, w3, w2, layer, s1, s3, s2, final_g):
    hp, eidx8, rank8, gate8, counts = _route(xa, na_rows, xb, g, modt, rows_per_mod,
                                             router_w.T, router_bias.reshape(-1, 1), tm=256)
    counts = counts[:, 0].astype(jnp.int32)
    starts = _cumsum_small(counts) - counts
    dest8 = _take_small(starts, eidx8) + rank8
    xs = _dispatch(hp, dest8, tm=256)
    ys = _experts(xs, counts, w1, w3, w2, layer, tmx=256)
    return _combine(ys, dest8, gate8.T, hp, xa, na_rows, xb, modt, rows_per_mod,
                    s1.astype(_BF16), s3.astype(_BF16), s2.astype(_BF16), final_g, tm=128)


def kernel(x, c, ctx, c_ctx, ada_w, ada_b, norm_g, fourier_wo, fourier_bo, da_wqkv, da_wo, da_lambda, da_subln_g, router_w, router_bias, exp_w1, exp_w3, exp_w2, shared_w1, shared_w3, shared_w2, final_g):
    b, s, d = x.shape
    n_ctx = ctx.shape[1]
    depth = ada_w.shape[0]
    assert depth == 2 and b + 1 <= 8 and s % _GRID_W == 0
    nl, nc = b * s, b * n_ctx

    cond8 = jnp.zeros((8, d), _F32).at[:b].set(c).at[b].set(c_ctx)
    mods = _adaln(cond8, ada_w, ada_b).reshape(depth, 8, _N_MOD, d)

    modt = mods[0, :b + 1]
    g_mix, g_moe = norm_g[0, 0:1], norm_g[0, 1:2]
    n2 = 64
    n1 = s // n2
    chan, m1, m2 = _dft_tables(s, n1, n2, d // _FOURIER_GROUPS)
    wo = fourier_wo[0].astype(_BF16)
    bo = fourier_bo[0:1]

    yl = _normmod_chandft(x, g_mix, modt, True, chan, tm=256)
    fl = _seq_dft(yl, m1, m2, n1, n2, tn=256)
    xl = _mm(fl.reshape(nl, d), wo, n_out=d, bias=bo, res=x.reshape(nl, d), gate=modt[:, 2:3], rows_per_gate=s,
             tm=1024, tn=512)

    yc = _normmod_chandft(ctx, g_mix, modt[b:b + 1], False, chan, tm=n_ctx)
    fc = _ctx_seq_dft(yc, tn=512)
    xc = _mm(fc.reshape(nc, d), wo, n_out=d, bias=bo, res=ctx.reshape(nc, d), gate=modt[b:b + 1, 2:3],
             rows_per_gate=nc, tm=n_ctx, tn=512)

    x_all = _moe(xl, nl, xc, g_moe, modt, s, router_w[0], router_bias[0], exp_w1, exp_w3, exp_w2, 0,
                 shared_w1[0], shared_w3[0], shared_w2[0], None)

    modt = mods[1, :b + 1]
    g_mix, g_moe = norm_g[1, 0:1], norm_g[1, 1:2]
    lam_init = 0.8 - 0.6 * float(np.exp(-0.3 * 1))
    wqkv = _rope_friendly_qkv_weights(da_wqkv[0])
    cos, sin = _rope_tables(s)

    hl = _normmod(x_all, 0, nl, g_mix, modt, s, 0, tm=512)
    hc = _normmod(x_all, nl, nc, g_mix, modt, nc, b, tm=n_ctx)
    qk, vt = _qkv(hl, wqkv, cos, sin, s, tm=1024, tn=512)
    kvc = _mm(hc, wqkv, n_out=2 * d, w_col0=d, out_dtype=_BF16, tm=n_ctx, tn=512)
    o = _attention(qk.reshape(b, s, 2 * d), vt, kvc.reshape(b, n_ctx, 2 * d), kvc[:, d:].T, da_lambda[0],
                   da_subln_g[0], lam_init, tq=2048, tk=1024)
    xl = _mm(o.reshape(nl, d), da_wo[0].astype(_BF16), n_out=d, res=x_all, gate=modt[:, 2:3], rows_per_gate=s,
             tm=1024, tn=512)

    out = _moe(xl, nl, None, g_moe, modt, s, router_w[1], router_bias[1], exp_w1, exp_w3, exp_w2, 1,
               shared_w1[1], shared_w3[1], shared_w2[1], final_g.reshape(1, d))
    return out.reshape(b, s, d)
```

```python
import functools
import math

import numpy as np
import jax
import jax.numpy as jnp
from jax import lax
from jax.experimental import pallas as pl
from jax.experimental.pallas import tpu as pltpu

_GRID_W = 64
_N_MOD = 6
_NORM_EPS = 1e-6
_FOURIER_GROUPS = 8
_HEAD_DIM = 128
_V_DIM = 2 * _HEAD_DIM
_ROPE_BASE = 10000.0
_TOP_K = 8
_N_GROUPS = 8
_TOPK_GROUPS = 4
_ROUTED_SCALE = 2.5

_LANE = 128
_VMEM_LIMIT = 56 * 1024 * 1024

_F32 = jnp.float32
_BF16 = jnp.bfloat16
_U32 = jnp.uint32
_HI_MASK = np.uint32(0xFFFF0000)
_NEG_INF = float("-inf")


def _params(sem, **kw):
    return pltpu.CompilerParams(dimension_semantics=sem, vmem_limit_bytes=_VMEM_LIMIT, **kw)


def _pack_bf16_pair(lo, hi):
    ul = pltpu.bitcast(lo.astype(_BF16).astype(_F32), _U32)
    uh = pltpu.bitcast(hi.astype(_BF16).astype(_F32), _U32)
    return (ul >> 16) | (uh & _HI_MASK)


def _unpack_lo(u):
    return pltpu.bitcast(u << 16, _F32)


def _unpack_hi(u):
    return pltpu.bitcast(u & _HI_MASK, _F32)


def _rms_mod(x, g, shift, scale):
    ms = jnp.mean(x * x, axis=-1, keepdims=True)
    return x * lax.rsqrt(ms + _NORM_EPS) * g * (1.0 + scale) + shift


def _adaln_kernel(c_ref, w_ref, b_ref, o_ref):
    c = c_ref[...]
    s = (c * jax.nn.sigmoid(c)).astype(_BF16)
    o_ref[...] = jnp.dot(s, w_ref[...].astype(_BF16), preferred_element_type=_F32) + b_ref[...]


def _adaln(cond8, ada_w, ada_b):
    depth, d, n6 = ada_w.shape
    tn = min(512, n6)
    return pl.pallas_call(
        _adaln_kernel,
        grid=(depth, n6 // tn),
        in_specs=[pl.BlockSpec((8, d), lambda l, j: (0, 0)),
                  pl.BlockSpec((None, d, tn), lambda l, j: (l, 0, j)),
                  pl.BlockSpec((None, 1, tn), lambda l, j: (l, 0, j))],
        out_specs=pl.BlockSpec((None, 8, tn), lambda l, j: (l, 0, j)),
        out_shape=jax.ShapeDtypeStruct((depth, 8, n6), _F32),
        compiler_params=_params(("arbitrary", "arbitrary")),
    )(cond8, ada_w, ada_b.reshape(depth, 1, n6))


def _normmod_chandft_kernel(x_ref, g_ref, mod_ref, tab_ref, o_ref, *, groups):
    h = _rms_mod(x_ref[...], g_ref[...], mod_ref[0:1, :], mod_ref[1:2, :]).astype(_BF16)
    cg = h.shape[-1] // groups
    for g in range(groups):
        r = jnp.dot(h[:, g * cg:(g + 1) * cg], tab_ref[...], preferred_element_type=_F32)
        o_ref[:, g * cg:(g + 1) * cg] = _pack_bf16_pair(r[:, :cg], r[:, cg:])


def _normmod_chandft(x3, g, mod, mod_per_batch, tab, tm):
    b, t, d = x3.shape
    mod_map = (lambda bi, i: (bi, 0, 0)) if mod_per_batch else (lambda bi, i: (0, 0, 0))
    return pl.pallas_call(
        functools.partial(_normmod_chandft_kernel, groups=_FOURIER_GROUPS),
        grid=(b, t // tm),
        in_specs=[pl.BlockSpec((None, tm, d), lambda bi, i: (bi, i, 0)),
                  pl.BlockSpec((1, d), lambda bi, i: (0, 0)),
                  pl.BlockSpec((None, _N_MOD, d), mod_map),
                  pl.BlockSpec(tab.shape, lambda bi, i: (0, 0))],
        out_specs=pl.BlockSpec((None, tm, d), lambda bi, i: (bi, i, 0)),
        out_shape=jax.ShapeDtypeStruct((b, t, d), _U32),
        compiler_params=_params(("arbitrary", "arbitrary")),
    )(x3, g, mod, tab)


def _normmod_kernel(x_ref, g_ref, mod_ref, o_ref):
    o_ref[...] = _rms_mod(x_ref[...], g_ref[...], mod_ref[0:1, :], mod_ref[1:2, :]).astype(_BF16)


def _normmod(x2, row0, rows, g, mod, rows_per_mod, mod0, tm):
    d = x2.shape[1]
    t0 = row0 // tm
    tpm = rows_per_mod // tm
    return pl.pallas_call(
        _normmod_kernel,
        grid=(rows // tm,),
        in_specs=[pl.BlockSpec((tm, d), lambda i: (t0 + i, 0)),
                  pl.BlockSpec((1, d), lambda i: (0, 0)),
                  pl.BlockSpec((None, _N_MOD, d), lambda i: (mod0 + i // tpm, 0, 0))],
        out_specs=pl.BlockSpec((tm, d), lambda i: (i, 0)),
        out_shape=jax.ShapeDtypeStruct((rows, d), _BF16),
        compiler_params=_params(("arbitrary",)),
    )(x2, g, mod)


def _unpack_complex_rows(u):
    return jnp.concatenate([_unpack_lo(u), _unpack_hi(u)], axis=0).astype(_BF16)


def _seq_dft_kernel(*refs, n1, n2, n_col):
    y_refs = refs[:n_col]
    m1_ref, m2_ref, o_ref, a_sc, f_sc = refs[n_col:]

    def stage1(t1, carry):
        u = jnp.concatenate([y[pl.ds(t1, n2, stride=n1), :] for y in y_refs], axis=1)
        r = jnp.dot(m1_ref[t1], _unpack_complex_rows(u), preferred_element_type=_F32)
        packed = _pack_bf16_pair(r[:n2], r[n2:])
        for c in range(n_col):
            a_sc[c, pl.ds(t1, n2, stride=n1), :] = packed[:, c * _LANE:(c + 1) * _LANE]
        return carry

    lax.fori_loop(0, n1, stage1, 0, unroll=4)

    def stage2(k2, carry):
        rows = pl.ds(pl.multiple_of(k2 * n1, n1), n1)
        u = jnp.concatenate([a_sc[c, rows, :] for c in range(n_col)], axis=1)
        r = jnp.dot(m2_ref[...], _unpack_complex_rows(u), preferred_element_type=_F32)
        for c in range(n_col):
            f_sc[c, pl.ds(k2, n1, stride=n2), :] = r[:, c * _LANE:(c + 1) * _LANE]
        return carry

    lax.fori_loop(0, n2, stage2, 0, unroll=4)
    for c in range(n_col):
        o_ref[:, c * _LANE:(c + 1) * _LANE] = f_sc[c].astype(_BF16)


def _cos_sin(n, rows, cols):
    ang = 2.0 * np.pi * ((np.arange(rows, dtype=np.int64)[:, None] * np.arange(cols, dtype=np.int64)[None, :]) % n) / n
    return np.cos(ang), np.sin(ang)


def _dft_tables(t, n1, n2, cg):
    cc, sc = _cos_sin(cg, cg, cg)
    chan = np.concatenate([cc, -sc], axis=1) / math.sqrt(cg)
    k2 = np.arange(n2, dtype=np.int64)
    idx = (k2[None, :, None] * k2[None, None, :] * n1 + k2[None, :, None] * np.arange(n1, dtype=np.int64)[:, None, None]) % t
    gr = np.cos(2.0 * np.pi * idx / t) / math.sqrt(t)
    gi = -np.sin(2.0 * np.pi * idx / t) / math.sqrt(t)
    m1 = np.concatenate([np.concatenate([gr, -gi], axis=2), np.concatenate([gi, gr], axis=2)], axis=1)
    c1, s1 = _cos_sin(n1, n1, n1)
    m2 = np.concatenate([c1, s1], axis=1)
    return jnp.asarray(chan, _BF16), jnp.asarray(m1, _BF16), jnp.asarray(m2, _BF16)


def _seq_dft(y, m1, m2, n1, n2, tn):
    b, t, d = y.shape
    n_col = tn // _LANE
    col_spec = lambda c: pl.BlockSpec((None, t, _LANE), lambda bi, j: (bi, 0, n_col * j + c))
    return pl.pallas_call(
        functools.partial(_seq_dft_kernel, n1=n1, n2=n2, n_col=n_col),
        grid=(b, d // tn),
        in_specs=[col_spec(c) for c in range(n_col)]
        + [pl.BlockSpec(m1.shape, lambda bi, j: (0, 0, 0)),
           pl.BlockSpec(m2.shape, lambda bi, j: (0, 0))],
        out_specs=pl.BlockSpec((None, t, tn), lambda bi, j: (bi, 0, j)),
        out_shape=jax.ShapeDtypeStruct((b, t, d), _BF16),
        scratch_shapes=[pltpu.VMEM((n_col, t, _LANE), _U32), pltpu.VMEM((n_col, t, _LANE), _F32)],
        compiler_params=_params(("arbitrary", "arbitrary")),
    )(*([y] * n_col), m1, m2)


def _ctx_dft_kernel(tab_ref, y_ref, o_ref):
    o_ref[...] = jnp.dot(tab_ref[...], _unpack_complex_rows(y_ref[...]), preferred_element_type=_F32).astype(_BF16)


def _ctx_seq_dft(y, tn):
    b, t, d = y.shape
    c, s = _cos_sin(t, t, t)
    tab = jnp.asarray(np.concatenate([c, s], axis=1) / math.sqrt(t), _BF16)
    return pl.pallas_call(
        _ctx_dft_kernel,
        grid=(b, d // tn),
        in_specs=[pl.BlockSpec(tab.shape, lambda bi, j: (0, 0)),
                  pl.BlockSpec((None, t, tn), lambda bi, j: (bi, 0, j))],
        out_specs=pl.BlockSpec((None, t, tn), lambda bi, j: (bi, 0, j)),
        out_shape=jax.ShapeDtypeStruct((b, t, d), _BF16),
        compiler_params=_params(("arbitrary", "arbitrary")),
    )(tab, y)


def _mm_kernel(*refs, has_bias, has_res):
    a_ref, w_ref = refs[0], refs[1]
    o_ref = refs[-1]
    acc = jnp.dot(a_ref[...], w_ref[...], preferred_element_type=_F32)
    k = 2
    if has_bias:
        acc = acc + refs[k][...]
        k += 1
    if has_res:
        acc = refs[k][...] + refs[k + 1][...] * acc
    o_ref[...] = acc.astype(o_ref.dtype)


def _mm(a, w, *, n_out, w_col0=0, bias=None, res=None, res_row0=0, gate=None, rows_per_gate=None, gate0=0,
        out_dtype=_F32, tm, tn):
    m, k = a.shape
    wj0 = w_col0 // tn
    in_specs = [pl.BlockSpec((tm, k), lambda i, j: (i, 0)),
                pl.BlockSpec((k, tn), lambda i, j: (0, wj0 + j))]
    args = [a, w]
    if bias is not None:
        in_specs.append(pl.BlockSpec((1, tn), lambda i, j: (0, j)))
        args.append(bias)
    if res is not None:
        r0 = res_row0 // tm
        tpg = rows_per_gate // tm
        in_specs.append(pl.BlockSpec((tm, tn), lambda i, j: (r0 + i, j)))
        in_specs.append(pl.BlockSpec((None, 1, tn), lambda i, j: (gate0 + i // tpg, 0, j)))
        args += [res, gate]
    return pl.pallas_call(
        functools.partial(_mm_kernel, has_bias=bias is not None, has_res=res is not None),
        grid=(m // tm, n_out // tn),
        in_specs=in_specs,
        out_specs=pl.BlockSpec((tm, tn), lambda i, j: (i, j)),
        out_shape=jax.ShapeDtypeStruct((m, n_out), out_dtype),
        compiler_params=_params(("arbitrary", "arbitrary")),
    )(*args)


def _qkv_kernel(a_ref, w_ref, cos_ref, sin_ref, o_ref, vt_ref, *, nq, nk, scale):
    j = pl.program_id(1)
    acc = jnp.dot(a_ref[...], w_ref[...], preferred_element_type=_F32)
    tn = acc.shape[1]

    @pl.when(j >= nq + nk)
    def _():
        vt_ref[...] = acc.T.astype(_BF16)

    @pl.when(j < nq + nk)
    def _():
        cos = cos_ref[...]
        sin = sin_ref[...]
        sc = jnp.where(j < nq, scale, 1.0).astype(_F32)
        for u in range(tn // _HEAD_DIM):
            x = acc[:, u * _HEAD_DIM:(u + 1) * _HEAD_DIM]
            sw = pltpu.roll(x, _HEAD_DIM // 2, 1)
            o_ref[:, u * _HEAD_DIM:(u + 1) * _HEAD_DIM] = ((x * cos + sw * sin) * sc).astype(_BF16)


def _rope_tables(s):
    n_freq = _HEAD_DIM // 4
    freqs = _ROPE_BASE ** (-jnp.arange(n_freq, dtype=_F32) / n_freq)
    pos = jnp.arange(s, dtype=jnp.int32)
    ang_r = (pos // _GRID_W).astype(_F32)[:, None] * freqs[None, :]
    ang_c = (pos % _GRID_W).astype(_F32)[:, None] * freqs[None, :]
    cos = jnp.concatenate([jnp.cos(ang_r), jnp.cos(ang_c)] * 2, axis=1)
    sin = jnp.concatenate([-jnp.sin(ang_r), -jnp.sin(ang_c), jnp.sin(ang_r), jnp.sin(ang_c)], axis=1)
    return cos, sin


def _prep_wqkv_kernel(w_ref, p_ref, o_ref, *, nqk):
    w = w_ref[...].astype(_BF16)

    @pl.when(pl.program_id(0) < nqk)
    def _():
        o_ref[...] = jnp.dot(w, p_ref[...], preferred_element_type=_F32).astype(_BF16)

    @pl.when(pl.program_id(0) >= nqk)
    def _():
        o_ref[...] = w


def _rope_friendly_qkv_weights(wqkv, tn):
    d = wqkv.shape[0]
    q4 = _HEAD_DIM // 4
    src = np.arange(tn).reshape(tn // _HEAD_DIM, 2, 2, q4).swapaxes(1, 2).reshape(tn)
    perm = np.zeros((tn, tn), np.float32)
    perm[src, np.arange(tn)] = 1.0
    return pl.pallas_call(
        functools.partial(_prep_wqkv_kernel, nqk=2 * d // tn),
        grid=(3 * d // tn,),
        in_specs=[pl.BlockSpec((d, tn), lambda j: (0, j)),
                  pl.BlockSpec((tn, tn), lambda j: (0, 0))],
        out_specs=pl.BlockSpec((d, tn), lambda j: (0, j)),
        out_shape=jax.ShapeDtypeStruct((d, 3 * d), _BF16),
        compiler_params=_params(("arbitrary",)),
    )(wqkv, jnp.asarray(perm, _BF16))


def _qkv(a, w, cos, sin, s, tm, tn):
    m, k = a.shape
    d = w.shape[1] // 3
    tps = s // tm
    nqk = 2 * d // tn
    return pl.pallas_call(
        functools.partial(_qkv_kernel, nq=d // tn, nk=d // tn, scale=_HEAD_DIM ** -0.5 * math.log2(math.e)),
        grid=(m // tm, 3 * d // tn),
        in_specs=[pl.BlockSpec((tm, k), lambda i, j: (i, 0)),
                  pl.BlockSpec((k, tn), lambda i, j: (0, j)),
                  pl.BlockSpec((tm, _HEAD_DIM), lambda i, j: (i % tps, 0)),
                  pl.BlockSpec((tm, _HEAD_DIM), lambda i, j: (i % tps, 0))],
        out_specs=[pl.BlockSpec((tm, tn), lambda i, j: (i, jnp.minimum(j, nqk - 1))),
                   pl.BlockSpec((tn, tm), lambda i, j: (jnp.maximum(j - nqk, 0), i))],
        out_shape=[jax.ShapeDtypeStruct((m, 2 * d), _BF16), jax.ShapeDtypeStruct((d, m), _BF16)],
        compiler_params=_params(("arbitrary", "arbitrary")),
    )(a, w, cos, sin)


def _attn_kernel(q_ref, k_ref, vt_ref, kc_ref, vct_ref, lam_ref, g_ref, o_ref,
                 s0, s1, p0, p1, m0, m1, a0, a1, x0, x1, acc0, acc1, *, tk, n_chunks, lam_init):
    s_sc, p_sc, m_sc, a_sc, x_sc, acc_sc = (s0, s1), (p0, p1), (m0, m1), (a0, a1), (x0, x1), (acc0, acc1)
    q = q_ref[...]
    qs = (q[:, :_HEAD_DIM], q[:, _HEAD_DIM:])
    dn = (((1,), (1,)), ((), ()))

    def scores(kb, c):
        return lax.dot_general(kb[:, c * _HEAD_DIM:(c + 1) * _HEAD_DIM], qs[c], dn, preferred_element_type=_F32)

    def scores_to_scratch(kb, c):
        st = scores(kb, c)
        s_sc[c][...] = st
        x_sc[c][...] = jnp.max(st, axis=0, keepdims=True)

    def chunk(i):
        return pl.ds(i * tk, tk) if isinstance(i, int) else pl.ds(pl.multiple_of(i * tk, tk), tk)

    def k_chunk(i):
        return k_ref[chunk(i), :]

    def vt_chunk(i):
        return vt_ref[:, chunk(i)]

    dv = vt_ref.shape[0]

    def with_ones(vtb):
        return jnp.concatenate([vtb, jnp.ones((16, vtb.shape[1]), _BF16)], axis=0)

    kcb = kc_ref[...]
    vctb = with_ones(vct_ref[...])
    st_ctx = [scores(kcb, c) for c in range(2)]
    scores_to_scratch(k_chunk(0), 0)
    for c in range(2):
        m = jnp.max(st_ctx[c], axis=0, keepdims=True)
        p = jnp.exp2(st_ctx[c] - m)
        m_sc[c][...] = m
        acc_sc[c][...] = jnp.dot(vctb, p.astype(_BF16), preferred_element_type=_F32)
        a_sc[c][...] = jnp.ones(a_sc[c].shape, _F32)
    p_sc[1][...] = jnp.zeros(p_sc[1].shape, _BF16)

    def softmax(c):
        m_old = m_sc[c][...]
        m_new = jnp.maximum(m_old, x_sc[c][...])
        a_sc[c][...] = jnp.exp2(m_old - m_new)
        m_sc[c][...] = m_new
        p_sc[c][...] = jnp.exp2(s_sc[c][...] - m_new).astype(_BF16)

    def pv(c, i):
        acc_sc[c][...] = (a_sc[c][...] * acc_sc[c][...]
                          + jnp.dot(with_ones(vt_chunk(i)), p_sc[c][...], preferred_element_type=_F32))

    def body(i, carry):
        scores_to_scratch(k_chunk(i), 1)
        softmax(0)
        pv(1, jnp.maximum(i - 1, 0))
        scores_to_scratch(k_chunk(i + 1), 0)
        softmax(1)
        pv(0, i)
        return carry

    lax.fori_loop(0, n_chunks - 1, body, 0)
    last = n_chunks - 1
    scores_to_scratch(k_chunk(last), 1)
    softmax(0)
    pv(1, max(last - 1, 0))
    softmax(1)
    pv(0, last)
    pv(1, last)

    lp = lam_ref[...]
    lam = (jnp.exp(jnp.sum(lp[0:1] * lp[1:2], axis=-1, keepdims=True))
           - jnp.exp(jnp.sum(lp[2:3] * lp[3:4], axis=-1, keepdims=True)) + lam_init)
    ot = acc0[:dv, :] * (1.0 / acc0[dv:dv + 1, :]) - acc1[:dv, :] * (lam / acc1[dv:dv + 1, :])
    ms = jnp.mean(ot * ot, axis=0, keepdims=True)
    ot = ot * lax.rsqrt(ms + _NORM_EPS) * g_ref[...] * (1.0 - lam_init)
    o_ref[...] = ot.T.astype(_BF16)


def _attention(qk, vt, kvc, vct, lam_p, subln_g, lam_init, tq, tk):
    b, s, d2 = qk.shape
    d = d2 // 2
    h = d // _V_DIM
    c = kvc.shape[1]
    return pl.pallas_call(
        functools.partial(_attn_kernel, tk=tk, n_chunks=s // tk, lam_init=lam_init),
        grid=(b, h, s // tq),
        in_specs=[pl.BlockSpec((None, tq, _V_DIM), lambda bi, hi, qi: (bi, qi, hi)),
                  pl.BlockSpec((None, s, _V_DIM), lambda bi, hi, qi: (bi, 0, h + hi), pipeline_mode=pl.Buffered(1)),
                  pl.BlockSpec((_V_DIM, s), lambda bi, hi, qi: (hi, bi), pipeline_mode=pl.Buffered(1)),
                  pl.BlockSpec((None, c, _V_DIM), lambda bi, hi, qi: (bi, 0, hi)),
                  pl.BlockSpec((_V_DIM, c), lambda bi, hi, qi: (hi, bi)),
                  pl.BlockSpec(lam_p.shape, lambda bi, hi, qi: (0, 0)),
                  pl.BlockSpec((_V_DIM, 1), lambda bi, hi, qi: (0, 0))],
        out_specs=pl.BlockSpec((None, tq, _V_DIM), lambda bi, hi, qi: (bi, qi, hi)),
        out_shape=jax.ShapeDtypeStruct((b, s, d), _BF16),
        scratch_shapes=([pltpu.VMEM((tk, tq), _F32)] * 2 + [pltpu.VMEM((tk, tq), _BF16)] * 2
                        + [pltpu.VMEM((1, tq), _F32)] * 6 + [pltpu.VMEM((_V_DIM + 16, tq), _F32)] * 2),
        compiler_params=_params(("arbitrary", "arbitrary", "arbitrary")),
    )(qk, qk, vt, kvc, vct, lam_p, subln_g.reshape(_V_DIM, 1))


def _route_kernel(*refs, na, two_inputs, n_exp):
    if two_inputs:
        xa_ref, xb_ref = refs[0], refs[1]
        refs = refs[2:]
    else:
        xa_ref, xb_ref = refs[0], None
        refs = refs[1:]
    g_ref, mod_ref, wt_ref, bias_ref, hp_ref, eidx_ref, rank_ref, gate_ref, cnt_ref, carry_sc = refs
    i = pl.program_id(0)

    @pl.when(i == 0)
    def _():
        carry_sc[...] = jnp.zeros(carry_sc.shape, _F32)

    x = xa_ref[...]
    if two_inputs:
        x = jnp.where(i < na, x, xb_ref[...])
    tm, d = x.shape
    h = _rms_mod(x, g_ref[...], mod_ref[3:4, :], mod_ref[4:5, :])
    hp_ref[...] = _pack_bf16_pair(h[:, :d // 2], h[:, d // 2:])

    h_hi = h.astype(_BF16)
    h_lo = (h - h_hi.astype(_F32)).astype(_BF16)
    w = wt_ref[...]
    w_hi = w.astype(_BF16)
    w_lo = (w - w_hi.astype(_F32)).astype(_BF16)
    dn = (((1,), (1,)), ((), ()))
    logits = (lax.dot_general(w_hi, h_hi, dn, preferred_element_type=_F32)
              + lax.dot_general(w_lo, h_hi, dn, preferred_element_type=_F32)
              + lax.dot_general(w_hi, h_lo, dn, preferred_element_type=_F32))
    scores = jax.nn.sigmoid(logits)
    sel = scores + bias_ref[...]

    pg = n_exp // _N_GROUPS
    shp = (_N_GROUPS, pg, tm)
    sel3 = sel.reshape(shp)
    scores3 = scores.reshape(shp)
    midx = lax.broadcasted_iota(jnp.int32, shp, 1).astype(_F32)
    gidx = lax.broadcasted_iota(jnp.int32, (_N_GROUPS, 1, tm), 0).astype(_F32)
    eid3 = lax.broadcasted_iota(jnp.int32, shp, 0).astype(_F32) * pg + midx

    m1 = jnp.max(sel3, axis=1, keepdims=True)
    f1 = jnp.min(jnp.where(sel3 == m1, midx, float(pg)), axis=1, keepdims=True)
    m2 = jnp.max(jnp.where(midx == f1, _NEG_INF, sel3), axis=1, keepdims=True)
    grp = m1 + m2

    gsel = jnp.zeros((_N_GROUPS, 1, tm), _F32)
    for _ in range(_TOPK_GROUPS):
        m = jnp.max(grp, axis=0, keepdims=True)
        f = jnp.min(jnp.where(grp == m, gidx, float(_N_GROUPS)), axis=0, keepdims=True)
        hit = gidx == f
        gsel = jnp.where(hit, 1.0, gsel)
        grp = jnp.where(hit, _NEG_INF, grp)
    cur = jnp.where(gsel > 0.0, sel3, _NEG_INF)

    chosen = jnp.zeros(shp, _F32)
    firsts = []
    for _ in range(_TOP_K):
        m = jnp.max(jnp.max(cur, axis=1, keepdims=True), axis=0, keepdims=True)
        f = jnp.min(jnp.min(jnp.where(cur == m, eid3, float(n_exp)), axis=1, keepdims=True), axis=0, keepdims=True)
        hit = eid3 == f
        chosen = jnp.where(hit, 1.0, chosen)
        cur = jnp.where(hit, _NEG_INF, cur)
        firsts.append(f)

    wsel = scores3 * chosen
    wsum = jnp.sum(jnp.sum(wsel, axis=1, keepdims=True), axis=0, keepdims=True)
    gates3 = wsel / wsum * _ROUTED_SCALE

    ch2 = chosen.reshape(n_exp, tm)
    upper = (lax.broadcasted_iota(jnp.int32, (tm, tm), 0) < lax.broadcasted_iota(jnp.int32, (tm, tm), 1))
    prefix = jnp.dot(ch2.astype(_BF16), jnp.where(upper, 1.0, 0.0).astype(_BF16), preferred_element_type=_F32)
    carry = carry_sc[...]
    rank3 = (carry + prefix).reshape(shp)
    new_carry = carry + jnp.sum(ch2, axis=-1, keepdims=True)
    carry_sc[...] = new_carry
    cnt_ref[...] = new_carry

    for r in range(_TOP_K):
        hit = eid3 == firsts[r]
        rk = jnp.sum(jnp.sum(jnp.where(hit, rank3, 0.0), axis=1, keepdims=True), axis=0, keepdims=True)
        gt = jnp.sum(jnp.sum(jnp.where(hit, gates3, 0.0), axis=1, keepdims=True), axis=0, keepdims=True)
        eidx_ref[r:r + 1, :] = firsts[r].reshape(1, tm).astype(jnp.int32)
        rank_ref[r:r + 1, :] = rk.reshape(1, tm).astype(jnp.int32)
        gate_ref[r:r + 1, :] = gt.reshape(1, tm)


def _route(xa, na_rows, xb, g, modt, rows_per_mod, wt, bias, tm):
    d = xa.shape[1]
    n_exp = wt.shape[0]
    na = na_rows // tm
    nb = 0 if xb is None else xb.shape[0] // tm
    n = (na + nb) * tm
    tpm = rows_per_mod // tm
    n_mod = modt.shape[0]
    in_specs = [pl.BlockSpec((tm, d), lambda i: (jnp.minimum(i, na - 1), 0))]
    args = [xa]
    if xb is not None:
        in_specs.append(pl.BlockSpec((tm, d), lambda i: (jnp.maximum(i - na, 0), 0)))
        args.append(xb)
    in_specs += [pl.BlockSpec((1, d), lambda i: (0, 0)),
                 pl.BlockSpec((None, _N_MOD, d), lambda i: (jnp.minimum(i // tpm, n_mod - 1), 0, 0)),
                 pl.BlockSpec((n_exp, d), lambda i: (0, 0)),
                 pl.BlockSpec((n_exp, 1), lambda i: (0, 0))]
    args += [g, modt, wt, bias]
    return pl.pallas_call(
        functools.partial(_route_kernel, na=na, two_inputs=xb is not None, n_exp=n_exp),
        grid=(na + nb,),
        in_specs=in_specs,
        out_specs=[pl.BlockSpec((tm, d // 2), lambda i: (i, 0)),
                   pl.BlockSpec((_TOP_K, tm), lambda i: (0, i)),
                   pl.BlockSpec((_TOP_K, tm), lambda i: (0, i)),
                   pl.BlockSpec((_TOP_K, tm), lambda i: (0, i)),
                   pl.BlockSpec((n_exp, 1), lambda i: (0, 0))],
        out_shape=[jax.ShapeDtypeStruct((n, d // 2), _U32),
                   jax.ShapeDtypeStruct((_TOP_K, n), jnp.int32),
                   jax.ShapeDtypeStruct((_TOP_K, n), jnp.int32),
                   jax.ShapeDtypeStruct((_TOP_K, n), _F32),
                   jax.ShapeDtypeStruct((n_exp, 1), _F32)],
        scratch_shapes=[pltpu.VMEM((n_exp, 1), _F32)],
        compiler_params=_params(("arbitrary",)),
    )(*args)


def _dispatch_kernel(dest_ref, hp_ref, xs_ref, sem):
    tm = hp_ref.shape[0]

    def row_copy(n, k):
        return pltpu.make_async_copy(hp_ref.at[pl.ds(n, 1), :], xs_ref.at[pl.ds(dest_ref[k, n], 1), :], sem)

    def body(n, carry):
        for k in range(_TOP_K):
            row_copy(n, k).start()
        return carry

    lax.fori_loop(0, tm, body, 0, unroll=4)
    for _ in range(_TOP_K):
        pltpu.make_async_copy(hp_ref, xs_ref.at[pl.ds(0, tm), :], sem).wait()


def _dispatch(hp, dest8, tm):
    n, dh = hp.shape
    return pl.pallas_call(
        _dispatch_kernel,
        grid=(n // tm,),
        in_specs=[pl.BlockSpec((_TOP_K, tm), lambda i: (0, i), memory_space=pltpu.SMEM),
                  pl.BlockSpec((tm, dh), lambda i: (i, 0))],
        out_specs=pl.BlockSpec(memory_space=pl.ANY),
        out_shape=jax.ShapeDtypeStruct((n * _TOP_K, dh), _U32),
        scratch_shapes=[pltpu.SemaphoreType.DMA(())],
        compiler_params=_params(("arbitrary",), disable_bounds_checks=True),
    )(dest8, hp)


def _swiglu_packed(xu, w1, w3, w2):
    dh = xu.shape[1]
    x_lo = _unpack_lo(xu).astype(_BF16)
    x_hi = _unpack_hi(xu).astype(_BF16)
    h1 = (jnp.dot(x_lo, w1[:dh], preferred_element_type=_F32) + jnp.dot(x_hi, w1[dh:], preferred_element_type=_F32))
    h3 = (jnp.dot(x_lo, w3[:dh], preferred_element_type=_F32) + jnp.dot(x_hi, w3[dh:], preferred_element_type=_F32))
    a = (h1 * jax.nn.sigmoid(h1) * h3).astype(_BF16)
    return jnp.dot(a, w2, preferred_element_type=_F32)


def _expert_kernel(tile_ref, exp_ref, lo_ref, hi_ref, first_ref, newexp_ref, valid_ref,
                   x_ref, w1_ref, w3_ref, w2_ref, y_ref, w1b, w3b, w2b):
    v = pl.program_id(0)

    @pl.when(newexp_ref[v] == 1)
    def _():
        w1b[...] = w1_ref[...].astype(_BF16)
        w3b[...] = w3_ref[...].astype(_BF16)
        w2b[...] = w2_ref[...].astype(_BF16)

    @pl.when(valid_ref[v] == 1)
    def _():
        tmx, dh = x_ref.shape
        y = _swiglu_packed(x_ref[...], w1b[...], w3b[...], w2b[...])
        packed = _pack_bf16_pair(y[:, :dh], y[:, dh:])
        rows = lax.broadcasted_iota(jnp.int32, (tmx, 1), 0)
        mine = (rows >= lo_ref[v]) & (rows < hi_ref[v])

        @pl.when(first_ref[v] == 1)
        def _():
            y_ref[...] = jnp.where(mine, packed, jnp.zeros_like(packed))

        @pl.when(first_ref[v] == 0)
        def _():
            y_ref[...] = jnp.where(mine, packed, y_ref[...])


def _cumsum_small(v):
    n = v.shape[0]
    tri = jnp.arange(n)[:, None] >= jnp.arange(n)[None, :]
    return jnp.sum(jnp.where(tri, v[None, :], 0), axis=1).astype(jnp.int32)


def _take_small(table, idx):
    hot = idx[..., None] == jnp.arange(table.shape[0], dtype=jnp.int32)
    return jnp.sum(jnp.where(hot, table, 0), axis=-1).astype(jnp.int32)


def _expert_visits(counts, n_rows, tmx):
    n_exp = counts.shape[0]
    n_tiles = n_rows // tmx
    n_vis = n_tiles + n_exp - 1
    ends = _cumsum_small(counts)
    starts = ends - counts
    t_first = starts // tmx
    t_last = jnp.where(counts > 0, (ends - 1) // tmx, t_first - 1)
    nv = t_last - t_first + 1
    v_end = _cumsum_small(nv)
    v_start = v_end - nv
    total = v_end[-1]
    v = jnp.arange(n_vis, dtype=jnp.int32)
    vc = jnp.minimum(v, total - 1)
    e = jnp.sum((v_end[None, :] <= vc[:, None]).astype(jnp.int32), axis=1)
    tile = _take_small(t_first, e) + (vc - _take_small(v_start, e))
    lo = jnp.clip(_take_small(starts, e) - tile * tmx, 0, tmx)
    hi = jnp.clip(_take_small(ends, e) - tile * tmx, 0, tmx)
    valid = (v < total).astype(jnp.int32)
    prev_tile = jnp.concatenate([jnp.full((1,), -1, jnp.int32), tile[:-1]])
    prev_e = jnp.concatenate([jnp.full((1,), -1, jnp.int32), e[:-1]])
    first = ((tile != prev_tile) & (valid == 1)).astype(jnp.int32)
    newexp = ((e != prev_e) & (valid == 1)).astype(jnp.int32)
    return tile, e, lo, hi, first, newexp, valid


def _experts(xs, counts, w1, w3, w2, layer, tmx):
    p, dh = xs.shape
    _, n_exp, d, f = w1.shape
    meta = _expert_visits(counts, p, tmx)
    n_vis = meta[0].shape[0]
    grid_spec = pltpu.PrefetchScalarGridSpec(
        num_scalar_prefetch=7,
        grid=(n_vis,),
        in_specs=[pl.BlockSpec((tmx, dh), lambda v, t, e, *_: (t[v], 0)),
                  pl.BlockSpec((None, None, d, f), lambda v, t, e, *_: (layer, e[v], 0, 0)),
                  pl.BlockSpec((None, None, d, f), lambda v, t, e, *_: (layer, e[v], 0, 0)),
                  pl.BlockSpec((None, None, f, d), lambda v, t, e, *_: (layer, e[v], 0, 0))],
        out_specs=pl.BlockSpec((tmx, dh), lambda v, t, e, *_: (t[v], 0)),
        scratch_shapes=[pltpu.VMEM((d, f), _BF16), pltpu.VMEM((d, f), _BF16), pltpu.VMEM((f, d), _BF16)],
    )
    return pl.pallas_call(
        _expert_kernel,
        grid_spec=grid_spec,
        out_shape=jax.ShapeDtypeStruct((p, dh), _U32),
        compiler_params=_params(("arbitrary",)),
    )(*meta, xs, w1, w3, w2)


def _combine_kernel(*refs, na, n_steps, two_inputs, final_norm):
    dcur_ref, dnext_ref, w_ref, hp_ref, xa_ref = refs[:5]
    refs = refs[5:]
    xb_ref = None
    if two_inputs:
        xb_ref, refs = refs[0], refs[1:]
    mod_ref, s1_ref, s3_ref, s2_ref = refs[:4]
    refs = refs[4:]
    fg_ref = None
    if final_norm:
        fg_ref, refs = refs[0], refs[1:]
    ys_ref, o_ref, gbuf, sems, ysh_sc = refs
    i = pl.program_id(0)
    slot = i % 2
    tm, dh = hp_ref.shape

    def row_copy(dref, sl, n, k):
        return pltpu.make_async_copy(ys_ref.at[pl.ds(dref[k, n], 1), :], gbuf.at[sl, k, pl.ds(n, 1), :], sems.at[sl])

    def issue(dref, sl):
        def body(n, carry):
            for k in range(_TOP_K):
                row_copy(dref, sl, n, k).start()
            return carry

        lax.fori_loop(0, tm, body, 0)

    def wait_slot(sl):
        for k in range(_TOP_K):
            pltpu.make_async_copy(ys_ref.at[pl.ds(0, tm), :], gbuf.at[sl, k], sems.at[sl]).wait()

    @pl.when(i == 0)
    def _():
        issue(dcur_ref, 0)

    ysh_sc[...] = _swiglu_packed(hp_ref[...], s1_ref[...], s3_ref[...], s2_ref[...])
    wait_slot(slot)

    def rows_body(r, carry):
        for t in range(8):
            for k in range(_TOP_K):
                row_copy(dnext_ref, 1 - slot, r * 8 + t, k).start()
        rows = pl.ds(pl.multiple_of(r * 8, 8), 8)
        acc_lo = ysh_sc[rows, :dh]
        acc_hi = ysh_sc[rows, dh:]
        for k in range(_TOP_K):
            u = gbuf[slot, k, rows, :]
            wk = w_ref[rows, k:k + 1]
            acc_lo = acc_lo + wk * _unpack_lo(u)
            acc_hi = acc_hi + wk * _unpack_hi(u)
        x = xa_ref[rows, :]
        if two_inputs:
            x = jnp.where(i < na, x, xb_ref[rows, :])
        out_lo = x[:, :dh] + mod_ref[5:6, :dh] * acc_lo
        out_hi = x[:, dh:] + mod_ref[5:6, dh:] * acc_hi
        if final_norm:
            ms = (jnp.sum(out_lo * out_lo, axis=-1, keepdims=True)
                  + jnp.sum(out_hi * out_hi, axis=-1, keepdims=True)) / (2 * dh)
            rs = lax.rsqrt(ms + _NORM_EPS)
            out_lo = out_lo * rs * fg_ref[:, :dh]
            out_hi = out_hi * rs * fg_ref[:, dh:]
        o_ref[rows, :dh] = out_lo
        o_ref[rows, dh:] = out_hi
        return carry

    lax.fori_loop(0, tm // 8, rows_body, 0)

    @pl.when(i == n_steps - 1)
    def _():
        wait_slot(1 - slot)


def _combine(ys, dest8, w8t, hp, xa, na_rows, xb, modt, rows_per_mod, s1, s3, s2, final_g, tm):
    n, dh = hp.shape
    d = 2 * dh
    f = s1.shape[1]
    n_steps = n // tm
    na = na_rows // tm
    tpm = rows_per_mod // tm
    n_mod = modt.shape[0]
    smem_spec = lambda fn: pl.BlockSpec((_TOP_K, tm), fn, memory_space=pltpu.SMEM)
    in_specs = [smem_spec(lambda i: (0, i)),
                smem_spec(lambda i: (0, jnp.minimum(i + 1, n_steps - 1))),
                pl.BlockSpec((tm, _TOP_K), lambda i: (i, 0)),
                pl.BlockSpec((tm, dh), lambda i: (i, 0)),
                pl.BlockSpec((tm, d), lambda i: (jnp.minimum(i, na - 1), 0))]
    args = [dest8, dest8, w8t, hp, xa]
    if xb is not None:
        in_specs.append(pl.BlockSpec((tm, d), lambda i: (jnp.maximum(i - na, 0), 0)))
        args.append(xb)
    in_specs += [pl.BlockSpec((None, _N_MOD, d), lambda i: (jnp.minimum(i // tpm, n_mod - 1), 0, 0)),
                 pl.BlockSpec((d, f), lambda i: (0, 0)),
                 pl.BlockSpec((d, f), lambda i: (0, 0)),
                 pl.BlockSpec((f, d), lambda i: (0, 0))]
    args += [modt, s1, s3, s2]
    if final_g is not None:
        in_specs.append(pl.BlockSpec((1, d), lambda i: (0, 0)))
        args.append(final_g)
    in_specs.append(pl.BlockSpec(memory_space=pl.ANY))
    args.append(ys)
    return pl.pallas_call(
        functools.partial(_combine_kernel, na=na, n_steps=n_steps, two_inputs=xb is not None,
                          final_norm=final_g is not None),
        grid=(n_steps,),
        in_specs=in_specs,
        out_specs=pl.BlockSpec((tm, d), lambda i: (i, 0)),
        out_shape=jax.ShapeDtypeStruct((n, d), _F32),
        scratch_shapes=[pltpu.VMEM((2, _TOP_K, tm, dh), _U32), pltpu.SemaphoreType.DMA((2,)),
                        pltpu.VMEM((tm, d), _F32)],
        compiler_params=_params(("arbitrary",), disable_bounds_checks=True),
    )(*args)


def _moe(xa, na_rows, xb, g, modt, rows_per_mod, router_w, router_bias, w1, w3, w2, layer, s1, s3, s2, final_g):
    hp, eidx8, rank8, gate8, counts = _route(xa, na_rows, xb, g, modt, rows_per_mod,
                                             router_w.T, router_bias.reshape(-1, 1), tm=256)
    counts = counts[:, 0].astype(jnp.int32)
    starts = _cumsum_small(counts) - counts
    dest8 = _take_small(starts, eidx8) + rank8
    xs = _dispatch(hp, dest8, tm=256)
    ys = _experts(xs, counts, w1, w3, w2, layer, tmx=256)
    return _combine(ys, dest8, gate8.T, hp, xa, na_rows, xb, modt, rows_per_mod,
                    s1.astype(_BF16), s3.astype(_BF16), s2.astype(_BF16), final_g, tm=128)


def kernel(x, c, ctx, c_ctx, ada_w, ada_b, norm_g, fourier_wo, fourier_bo, da_wqkv, da_wo, da_lambda, da_subln_g, router_w, router_bias, exp_w1, exp_w3, exp_w2, shared_w1, shared_w3, shared_w2, final_g):
    b, s, d = x.shape
    n_ctx = ctx.shape[1]
    depth = ada_w.shape[0]
    assert depth == 2 and b + 1 <= 8 and s % _GRID_W == 0
    assert s % 2048 == 0 and d % 1024 == 0 and n_ctx % 128 == 0 and d == _V_DIM * (d // _V_DIM)
    nl, nc = b * s, b * n_ctx

    cond8 = jnp.zeros((8, d), _F32).at[:b].set(c).at[b].set(c_ctx)
    mods = _adaln(cond8, ada_w, ada_b).reshape(depth, 8, _N_MOD, d)

    modt = mods[0, :b + 1]
    g_mix, g_moe = norm_g[0, 0:1], norm_g[0, 1:2]
    n2 = 64
    n1 = s // n2
    chan, m1, m2 = _dft_tables(s, n1, n2, d // _FOURIER_GROUPS)
    wo = fourier_wo[0].astype(_BF16)
    bo = fourier_bo[0:1]

    yl = _normmod_chandft(x, g_mix, modt, True, chan, tm=256)
    fl = _seq_dft(yl, m1, m2, n1, n2, tn=256)
    xl = _mm(fl.reshape(nl, d), wo, n_out=d, bias=bo, res=x.reshape(nl, d), gate=modt[:, 2:3], rows_per_gate=s,
             tm=1024, tn=512)

    yc = _normmod_chandft(ctx, g_mix, modt[b:b + 1], False, chan, tm=n_ctx)
    fc = _ctx_seq_dft(yc, tn=512)
    xc = _mm(fc.reshape(nc, d), wo, n_out=d, bias=bo, res=ctx.reshape(nc, d), gate=modt[b:b + 1, 2:3],
             rows_per_gate=nc, tm=n_ctx, tn=512)

    x_all = _moe(xl, nl, xc, g_moe, modt, s, router_w[0], router_bias[0], exp_w1, exp_w3, exp_w2, 0,
                 shared_w1[0], shared_w3[0], shared_w2[0], None)

    modt = mods[1, :b + 1]
    g_mix, g_moe = norm_g[1, 0:1], norm_g[1, 1:2]
    lam_init = 0.8 - 0.6 * float(np.exp(-0.3 * 1))
    wqkv = _rope_friendly_qkv_weights(da_wqkv[0], tn=512)
    cos, sin = _rope_tables(s)

    hl = _normmod(x_all, 0, nl, g_mix, modt, s, 0, tm=512)
    hc = _normmod(x_all, nl, nc, g_mix, modt, nc, b, tm=n_ctx)
    qk, vt = _qkv(hl, wqkv, cos, sin, s, tm=1024, tn=512)
    kvc = _mm(hc, wqkv, n_out=2 * d, w_col0=d, out_dtype=_BF16, tm=n_ctx, tn=512)
    o = _attention(qk.reshape(b, s, 2 * d), vt, kvc.reshape(b, n_ctx, 2 * d), kvc[:, d:].T, da_lambda[0],
                   da_subln_g[0], lam_init, tq=2048, tk=1024)
    xl = _mm(o.reshape(nl, d), da_wo[0].astype(_BF16), n_out=d, res=x_all, gate=modt[:, 2:3], rows_per_gate=s,
             tm=1024, tn=512)

    out = _moe(xl, nl, None, g_moe, modt, s, router_w[1], router_bias[1], exp_w1, exp_w3, exp_w2, 1,
               shared_w1[1], shared_w3[1], shared_w2[1], final_g.reshape(1, d))
    return out.reshape(b, s, d)
```

```python
import functools
import math

import numpy as np
import jax
import jax.numpy as jnp
from jax import lax
from jax.experimental import pallas as pl
from jax.experimental.pallas import tpu as pltpu

_GRID_W = 64
_N_MOD = 6
_NORM_EPS = 1e-6
_FOURIER_GROUPS = 8
_HEAD_DIM = 128
_V_DIM = 2 * _HEAD_DIM
_ROPE_BASE = 10000.0
_TOP_K = 8
_N_GROUPS = 8
_TOPK_GROUPS = 4
_ROUTED_SCALE = 2.5

_LANE = 128
_VMEM_LIMIT = 56 * 1024 * 1024

_F32 = jnp.float32
_BF16 = jnp.bfloat16
_U32 = jnp.uint32
_HI_MASK = np.uint32(0xFFFF0000)
_NEG_INF = float("-inf")


def _params(sem, **kw):
    return pltpu.CompilerParams(dimension_semantics=sem, vmem_limit_bytes=_VMEM_LIMIT, **kw)


def _pack_bf16_pair(lo, hi):
    ul = pltpu.bitcast(lo.astype(_BF16).astype(_F32), _U32)
    uh = pltpu.bitcast(hi.astype(_BF16).astype(_F32), _U32)
    return (ul >> 16) | (uh & _HI_MASK)


def _unpack_lo(u):
    return pltpu.bitcast(u << 16, _F32)


def _unpack_hi(u):
    return pltpu.bitcast(u & _HI_MASK, _F32)


def _rms_mod(x, g, shift, scale):
    ms = jnp.mean(x * x, axis=-1, keepdims=True)
    return x * lax.rsqrt(ms + _NORM_EPS) * g * (1.0 + scale) + shift


def _adaln_kernel(c_ref, w_ref, b_ref, o_ref):
    c = c_ref[...]
    s = (c * jax.nn.sigmoid(c)).astype(_BF16)
    o_ref[...] = jnp.dot(s, w_ref[...].astype(_BF16), preferred_element_type=_F32) + b_ref[...]


def _adaln(cond8, ada_w, ada_b):
    depth, d, n6 = ada_w.shape
    tn = min(512, n6)
    return pl.pallas_call(
        _adaln_kernel,
        grid=(depth, n6 // tn),
        in_specs=[pl.BlockSpec((8, d), lambda l, j: (0, 0)),
                  pl.BlockSpec((None, d, tn), lambda l, j: (l, 0, j)),
                  pl.BlockSpec((None, 1, tn), lambda l, j: (l, 0, j))],
        out_specs=pl.BlockSpec((None, 8, tn), lambda l, j: (l, 0, j)),
        out_shape=jax.ShapeDtypeStruct((depth, 8, n6), _F32),
        compiler_params=_params(("arbitrary", "arbitrary")),
    )(cond8, ada_w, ada_b.reshape(depth, 1, n6))


def _normmod_chandft_kernel(x_ref, g_ref, mod_ref, tab_ref, o_ref, *, groups):
    h = _rms_mod(x_ref[...], g_ref[...], mod_ref[0:1, :], mod_ref[1:2, :]).astype(_BF16)
    cg = h.shape[-1] // groups
    for g in range(groups):
        r = jnp.dot(h[:, g * cg:(g + 1) * cg], tab_ref[...], preferred_element_type=_F32)
        o_ref[:, g * cg:(g + 1) * cg] = _pack_bf16_pair(r[:, :cg], r[:, cg:])


def _normmod_chandft(x3, g, mod, mod_per_batch, tab, tm):
    b, t, d = x3.shape
    mod_map = (lambda bi, i: (bi, 0, 0)) if mod_per_batch else (lambda bi, i: (0, 0, 0))
    return pl.pallas_call(
        functools.partial(_normmod_chandft_kernel, groups=_FOURIER_GROUPS),
        grid=(b, t // tm),
        in_specs=[pl.BlockSpec((None, tm, d), lambda bi, i: (bi, i, 0)),
                  pl.BlockSpec((1, d), lambda bi, i: (0, 0)),
                  pl.BlockSpec((None, _N_MOD, d), mod_map),
                  pl.BlockSpec(tab.shape, lambda bi, i: (0, 0))],
        out_specs=pl.BlockSpec((None, tm, d), lambda bi, i: (bi, i, 0)),
        out_shape=jax.ShapeDtypeStruct((b, t, d), _U32),
        compiler_params=_params(("arbitrary", "arbitrary")),
    )(x3, g, mod, tab)


def _normmod_kernel(x_ref, g_ref, mod_ref, o_ref):
    o_ref[...] = _rms_mod(x_ref[...], g_ref[...], mod_ref[0:1, :], mod_ref[1:2, :]).astype(_BF16)


def _normmod(x2, row0, rows, g, mod, rows_per_mod, mod0, tm):
    d = x2.shape[1]
    t0 = row0 // tm
    tpm = rows_per_mod // tm
    return pl.pallas_call(
        _normmod_kernel,
        grid=(rows // tm,),
        in_specs=[pl.BlockSpec((tm, d), lambda i: (t0 + i, 0)),
                  pl.BlockSpec((1, d), lambda i: (0, 0)),
                  pl.BlockSpec((None, _N_MOD, d), lambda i: (mod0 + i // tpm, 0, 0))],
        out_specs=pl.BlockSpec((tm, d), lambda i: (i, 0)),
        out_shape=jax.ShapeDtypeStruct((rows, d), _BF16),
        compiler_params=_params(("arbitrary",)),
    )(x2, g, mod)


def _unpack_complex_rows(u):
    return jnp.concatenate([_unpack_lo(u), _unpack_hi(u)], axis=0).astype(_BF16)


def _seq_dft_kernel(*refs, n1, n2, n_col):
    y_refs = refs[:n_col]
    m1_ref, m2_ref, o_ref, a_sc, f_sc = refs[n_col:]

    def stage1(t1, carry):
        u = jnp.concatenate([y[pl.ds(t1, n2, stride=n1), :] for y in y_refs], axis=1)
        r = jnp.dot(m1_ref[t1], _unpack_complex_rows(u), preferred_element_type=_F32)
        packed = _pack_bf16_pair(r[:n2], r[n2:])
        for c in range(n_col):
            a_sc[c, pl.ds(t1, n2, stride=n1), :] = packed[:, c * _LANE:(c + 1) * _LANE]
        return carry

    lax.fori_loop(0, n1, stage1, 0, unroll=8)

    def stage2(k2, carry):
        rows = pl.ds(pl.multiple_of(k2 * n1, n1), n1)
        u = jnp.concatenate([a_sc[c, rows, :] for c in range(n_col)], axis=1)
        r = jnp.dot(m2_ref[...], _unpack_complex_rows(u), preferred_element_type=_F32)
        for c in range(n_col):
            f_sc[c, pl.ds(k2, n1, stride=n2), :] = r[:, c * _LANE:(c + 1) * _LANE]
        return carry

    lax.fori_loop(0, n2, stage2, 0, unroll=8)
    for c in range(n_col):
        o_ref[:, c * _LANE:(c + 1) * _LANE] = f_sc[c].astype(_BF16)


def _cos_sin(n, rows, cols):
    ang = 2.0 * np.pi * ((np.arange(rows, dtype=np.int64)[:, None] * np.arange(cols, dtype=np.int64)[None, :]) % n) / n
    return np.cos(ang), np.sin(ang)


def _dft_tables(t, n1, n2, cg):
    cc, sc = _cos_sin(cg, cg, cg)
    chan = np.concatenate([cc, -sc], axis=1) / math.sqrt(cg)
    k2 = np.arange(n2, dtype=np.int64)
    idx = (k2[None, :, None] * k2[None, None, :] * n1 + k2[None, :, None] * np.arange(n1, dtype=np.int64)[:, None, None]) % t
    gr = np.cos(2.0 * np.pi * idx / t) / math.sqrt(t)
    gi = -np.sin(2.0 * np.pi * idx / t) / math.sqrt(t)
    m1 = np.concatenate([np.concatenate([gr, -gi], axis=2), np.concatenate([gi, gr], axis=2)], axis=1)
    c1, s1 = _cos_sin(n1, n1, n1)
    m2 = np.concatenate([c1, s1], axis=1)
    return jnp.asarray(chan, _BF16), jnp.asarray(m1, _BF16), jnp.asarray(m2, _BF16)


def _seq_dft(y, m1, m2, n1, n2, tn):
    b, t, d = y.shape
    n_col = tn // _LANE
    col_spec = lambda c: pl.BlockSpec((None, t, _LANE), lambda bi, j: (bi, 0, n_col * j + c))
    return pl.pallas_call(
        functools.partial(_seq_dft_kernel, n1=n1, n2=n2, n_col=n_col),
        grid=(b, d // tn),
        in_specs=[col_spec(c) for c in range(n_col)]
        + [pl.BlockSpec(m1.shape, lambda bi, j: (0, 0, 0)),
           pl.BlockSpec(m2.shape, lambda bi, j: (0, 0))],
        out_specs=pl.BlockSpec((None, t, tn), lambda bi, j: (bi, 0, j)),
        out_shape=jax.ShapeDtypeStruct((b, t, d), _BF16),
        scratch_shapes=[pltpu.VMEM((n_col, t, _LANE), _U32), pltpu.VMEM((n_col, t, _LANE), _F32)],
        compiler_params=_params(("arbitrary", "arbitrary")),
    )(*([y] * n_col), m1, m2)


def _ctx_dft_kernel(tab_ref, y_ref, o_ref):
    o_ref[...] = jnp.dot(tab_ref[...], _unpack_complex_rows(y_ref[...]), preferred_element_type=_F32).astype(_BF16)


def _ctx_seq_dft(y, tn):
    b, t, d = y.shape
    c, s = _cos_sin(t, t, t)
    tab = jnp.asarray(np.concatenate([c, s], axis=1) / math.sqrt(t), _BF16)
    return pl.pallas_call(
        _ctx_dft_kernel,
        grid=(b, d // tn),
        in_specs=[pl.BlockSpec(tab.shape, lambda bi, j: (0, 0)),
                  pl.BlockSpec((None, t, tn), lambda bi, j: (bi, 0, j))],
        out_specs=pl.BlockSpec((None, t, tn), lambda bi, j: (bi, 0, j)),
        out_shape=jax.ShapeDtypeStruct((b, t, d), _BF16),
        compiler_params=_params(("arbitrary", "arbitrary")),
    )(tab, y)


def _mm_kernel(*refs, has_bias, has_res):
    a_ref, w_ref = refs[0], refs[1]
    o_ref = refs[-1]
    acc = jnp.dot(a_ref[...], w_ref[...], preferred_element_type=_F32)
    k = 2
    if has_bias:
        acc = acc + refs[k][...]
        k += 1
    if has_res:
        acc = refs[k][...] + refs[k + 1][...] * acc
    o_ref[...] = acc.astype(o_ref.dtype)


def _mm(a, w, *, n_out, w_col0=0, bias=None, res=None, res_row0=0, gate=None, rows_per_gate=None, gate0=0,
        out_dtype=_F32, tm, tn):
    m, k = a.shape
    wj0 = w_col0 // tn
    in_specs = [pl.BlockSpec((tm, k), lambda i, j: (i, 0)),
                pl.BlockSpec((k, tn), lambda i, j: (0, wj0 + j))]
    args = [a, w]
    if bias is not None:
        in_specs.append(pl.BlockSpec((1, tn), lambda i, j: (0, j)))
        args.append(bias)
    if res is not None:
        r0 = res_row0 // tm
        tpg = rows_per_gate // tm
        in_specs.append(pl.BlockSpec((tm, tn), lambda i, j: (r0 + i, j)))
        in_specs.append(pl.BlockSpec((None, 1, tn), lambda i, j: (gate0 + i // tpg, 0, j)))
        args += [res, gate]
    return pl.pallas_call(
        functools.partial(_mm_kernel, has_bias=bias is not None, has_res=res is not None),
        grid=(m // tm, n_out // tn),
        in_specs=in_specs,
        out_specs=pl.BlockSpec((tm, tn), lambda i, j: (i, j)),
        out_shape=jax.ShapeDtypeStruct((m, n_out), out_dtype),
        compiler_params=_params(("arbitrary", "arbitrary")),
    )(*args)


def _qkv_kernel(a_ref, w_ref, cos_ref, sin_ref, o_ref, vt_ref, *, nq, nk, scale):
    j = pl.program_id(1)
    acc = jnp.dot(a_ref[...], w_ref[...], preferred_element_type=_F32)
    tn = acc.shape[1]

    @pl.when(j >= nq + nk)
    def _():
        vt_ref[...] = acc.T.astype(_BF16)

    @pl.when(j < nq + nk)
    def _():
        cos = cos_ref[...]
        sin = sin_ref[...]
        sc = jnp.where(j < nq, scale, 1.0).astype(_F32)
        for u in range(tn // _HEAD_DIM):
            x = acc[:, u * _HEAD_DIM:(u + 1) * _HEAD_DIM]
            sw = pltpu.roll(x, _HEAD_DIM // 2, 1)
            o_ref[:, u * _HEAD_DIM:(u + 1) * _HEAD_DIM] = ((x * cos + sw * sin) * sc).astype(_BF16)


def _rope_tables(s):
    n_freq = _HEAD_DIM // 4
    freqs = _ROPE_BASE ** (-jnp.arange(n_freq, dtype=_F32) / n_freq)
    pos = jnp.arange(s, dtype=jnp.int32)
    ang_r = (pos // _GRID_W).astype(_F32)[:, None] * freqs[None, :]
    ang_c = (pos % _GRID_W).astype(_F32)[:, None] * freqs[None, :]
    cos = jnp.concatenate([jnp.cos(ang_r), jnp.cos(ang_c)] * 2, axis=1)
    sin = jnp.concatenate([-jnp.sin(ang_r), -jnp.sin(ang_c), jnp.sin(ang_r), jnp.sin(ang_c)], axis=1)
    return cos, sin


def _prep_wqkv_kernel(w_ref, p_ref, o_ref, *, nqk):
    w = w_ref[...].astype(_BF16)

    @pl.when(pl.program_id(0) < nqk)
    def _():
        o_ref[...] = jnp.dot(w, p_ref[...], preferred_element_type=_F32).astype(_BF16)

    @pl.when(pl.program_id(0) >= nqk)
    def _():
        o_ref[...] = w


def _rope_friendly_qkv_weights(wqkv, tn):
    d = wqkv.shape[0]
    q4 = _HEAD_DIM // 4
    src = np.arange(tn).reshape(tn // _HEAD_DIM, 2, 2, q4).swapaxes(1, 2).reshape(tn)
    perm = np.zeros((tn, tn), np.float32)
    perm[src, np.arange(tn)] = 1.0
    return pl.pallas_call(
        functools.partial(_prep_wqkv_kernel, nqk=2 * d // tn),
        grid=(3 * d // tn,),
        in_specs=[pl.BlockSpec((d, tn), lambda j: (0, j)),
                  pl.BlockSpec((tn, tn), lambda j: (0, 0))],
        out_specs=pl.BlockSpec((d, tn), lambda j: (0, j)),
        out_shape=jax.ShapeDtypeStruct((d, 3 * d), _BF16),
        compiler_params=_params(("arbitrary",)),
    )(wqkv, jnp.asarray(perm, _BF16))


def _qkv(a, w, cos, sin, s, tm, tn):
    m, k = a.shape
    d = w.shape[1] // 3
    tps = s // tm
    nqk = 2 * d // tn
    return pl.pallas_call(
        functools.partial(_qkv_kernel, nq=d // tn, nk=d // tn, scale=_HEAD_DIM ** -0.5 * math.log2(math.e)),
        grid=(m // tm, 3 * d // tn),
        in_specs=[pl.BlockSpec((tm, k), lambda i, j: (i, 0)),
                  pl.BlockSpec((k, tn), lambda i, j: (0, j)),
                  pl.BlockSpec((tm, _HEAD_DIM), lambda i, j: (i % tps, 0)),
                  pl.BlockSpec((tm, _HEAD_DIM), lambda i, j: (i % tps, 0))],
        out_specs=[pl.BlockSpec((tm, tn), lambda i, j: (i, jnp.minimum(j, nqk - 1))),
                   pl.BlockSpec((tn, tm), lambda i, j: (jnp.maximum(j - nqk, 0), i))],
        out_shape=[jax.ShapeDtypeStruct((m, 2 * d), _BF16), jax.ShapeDtypeStruct((d, m), _BF16)],
        compiler_params=_params(("arbitrary", "arbitrary")),
    )(a, w, cos, sin)


def _attn_kernel(q_ref, k_ref, vt_ref, kc_ref, vct_ref, lam_ref, g_ref, o_ref,
                 s0, s1, p0, p1, m0, m1, a0, a1, x0, x1, acc0, acc1, *, tk, n_chunks, lam_init):
    s_sc, p_sc, m_sc, a_sc, x_sc, acc_sc = (s0, s1), (p0, p1), (m0, m1), (a0, a1), (x0, x1), (acc0, acc1)
    q = q_ref[...]
    qs = (q[:, :_HEAD_DIM], q[:, _HEAD_DIM:])
    dn = (((1,), (1,)), ((), ()))

    def scores(kb, c):
        return lax.dot_general(kb[:, c * _HEAD_DIM:(c + 1) * _HEAD_DIM], qs[c], dn, preferred_element_type=_F32)

    def scores_to_scratch(kb, c):
        st = scores(kb, c)
        s_sc[c][...] = st
        x_sc[c][...] = jnp.max(st, axis=0, keepdims=True)

    def chunk(i):
        return pl.ds(i * tk, tk) if isinstance(i, int) else pl.ds(pl.multiple_of(i * tk, tk), tk)

    def k_chunk(i):
        return k_ref[chunk(i), :]

    def vt_chunk(i):
        return vt_ref[:, chunk(i)]

    dv = vt_ref.shape[0]

    def with_ones(vtb):
        return jnp.concatenate([vtb, jnp.ones((16, vtb.shape[1]), _BF16)], axis=0)

    kcb = kc_ref[...]
    vctb = with_ones(vct_ref[...])
    st_ctx = [scores(kcb, c) for c in range(2)]
    scores_to_scratch(k_chunk(0), 0)
    for c in range(2):
        m = jnp.max(st_ctx[c], axis=0, keepdims=True)
        p = jnp.exp2(st_ctx[c] - m)
        m_sc[c][...] = m
        acc_sc[c][...] = jnp.dot(vctb, p.astype(_BF16), preferred_element_type=_F32)
        a_sc[c][...] = jnp.ones(a_sc[c].shape, _F32)
    p_sc[1][...] = jnp.zeros(p_sc[1].shape, _BF16)

    def softmax(c):
        m_old = m_sc[c][...]
        m_new = jnp.maximum(m_old, x_sc[c][...])
        a_sc[c][...] = jnp.exp2(m_old - m_new)
        m_sc[c][...] = m_new
        p_sc[c][...] = jnp.exp2(s_sc[c][...] - m_new).astype(_BF16)

    def pv(c, i):
        acc_sc[c][...] = (a_sc[c][...] * acc_sc[c][...]
                          + jnp.dot(with_ones(vt_chunk(i)), p_sc[c][...], preferred_element_type=_F32))

    def body(i, carry):
        scores_to_scratch(k_chunk(i), 1)
        softmax(0)
        pv(1, jnp.maximum(i - 1, 0))
        scores_to_scratch(k_chunk(i + 1), 0)
        softmax(1)
        pv(0, i)
        return carry

    lax.fori_loop(0, n_chunks - 1, body, 0)
    last = n_chunks - 1
    scores_to_scratch(k_chunk(last), 1)
    softmax(0)
    pv(1, max(last - 1, 0))
    softmax(1)
    pv(0, last)
    pv(1, last)

    lp = lam_ref[...]
    lam = (jnp.exp(jnp.sum(lp[0:1] * lp[1:2], axis=-1, keepdims=True))
           - jnp.exp(jnp.sum(lp[2:3] * lp[3:4], axis=-1, keepdims=True)) + lam_init)
    ot = acc0[:dv, :] * (1.0 / acc0[dv:dv + 1, :]) - acc1[:dv, :] * (lam / acc1[dv:dv + 1, :])
    ms = jnp.mean(ot * ot, axis=0, keepdims=True)
    ot = ot * lax.rsqrt(ms + _NORM_EPS) * g_ref[...] * (1.0 - lam_init)
    o_ref[...] = ot.T.astype(_BF16)


def _attention(qk, vt, kvc, vct, lam_p, subln_g, lam_init, tq, tk):
    b, s, d2 = qk.shape
    d = d2 // 2
    h = d // _V_DIM
    c = kvc.shape[1]
    return pl.pallas_call(
        functools.partial(_attn_kernel, tk=tk, n_chunks=s // tk, lam_init=lam_init),
        grid=(b, h, s // tq),
        in_specs=[pl.BlockSpec((None, tq, _V_DIM), lambda bi, hi, qi: (bi, qi, hi)),
                  pl.BlockSpec((None, s, _V_DIM), lambda bi, hi, qi: (bi, 0, h + hi), pipeline_mode=pl.Buffered(1)),
                  pl.BlockSpec((_V_DIM, s), lambda bi, hi, qi: (hi, bi), pipeline_mode=pl.Buffered(1)),
                  pl.BlockSpec((None, c, _V_DIM), lambda bi, hi, qi: (bi, 0, hi)),
                  pl.BlockSpec((_V_DIM, c), lambda bi, hi, qi: (hi, bi)),
                  pl.BlockSpec(lam_p.shape, lambda bi, hi, qi: (0, 0)),
                  pl.BlockSpec((_V_DIM, 1), lambda bi, hi, qi: (0, 0))],
        out_specs=pl.BlockSpec((None, tq, _V_DIM), lambda bi, hi, qi: (bi, qi, hi)),
        out_shape=jax.ShapeDtypeStruct((b, s, d), _BF16),
        scratch_shapes=([pltpu.VMEM((tk, tq), _F32)] * 2 + [pltpu.VMEM((tk, tq), _BF16)] * 2
                        + [pltpu.VMEM((1, tq), _F32)] * 6 + [pltpu.VMEM((_V_DIM + 16, tq), _F32)] * 2),
        compiler_params=_params(("arbitrary", "arbitrary", "arbitrary")),
    )(qk, qk, vt, kvc, vct, lam_p, subln_g.reshape(_V_DIM, 1))


def _route_kernel(*refs, na, two_inputs, n_exp):
    if two_inputs:
        xa_ref, xb_ref = refs[0], refs[1]
        refs = refs[2:]
    else:
        xa_ref, xb_ref = refs[0], None
        refs = refs[1:]
    g_ref, mod_ref, wt_ref, bias_ref, hp_ref, eidx_ref, rank_ref, gate_ref, cnt_ref, carry_sc = refs
    i = pl.program_id(0)

    @pl.when(i == 0)
    def _():
        carry_sc[...] = jnp.zeros(carry_sc.shape, _F32)

    x = xa_ref[...]
    if two_inputs:
        x = jnp.where(i < na, x, xb_ref[...])
    tm, d = x.shape
    h = _rms_mod(x, g_ref[...], mod_ref[3:4, :], mod_ref[4:5, :])
    hp_ref[...] = _pack_bf16_pair(h[:, :d // 2], h[:, d // 2:])

    h_hi = h.astype(_BF16)
    h_lo = (h - h_hi.astype(_F32)).astype(_BF16)
    w = wt_ref[...]
    w_hi = w.astype(_BF16)
    w_lo = (w - w_hi.astype(_F32)).astype(_BF16)
    dn = (((1,), (1,)), ((), ()))
    logits = (lax.dot_general(w_hi, h_hi, dn, preferred_element_type=_F32)
              + lax.dot_general(w_lo, h_hi, dn, preferred_element_type=_F32)
              + lax.dot_general(w_hi, h_lo, dn, preferred_element_type=_F32))
    scores = jax.nn.sigmoid(logits)
    sel = scores + bias_ref[...]

    pg = n_exp // _N_GROUPS
    shp = (_N_GROUPS, pg, tm)
    sel3 = sel.reshape(shp)
    scores3 = scores.reshape(shp)
    midx = lax.broadcasted_iota(jnp.int32, shp, 1).astype(_F32)
    gidx = lax.broadcasted_iota(jnp.int32, (_N_GROUPS, 1, tm), 0).astype(_F32)
    eid3 = lax.broadcasted_iota(jnp.int32, shp, 0).astype(_F32) * pg + midx

    m1 = jnp.max(sel3, axis=1, keepdims=True)
    f1 = jnp.min(jnp.where(sel3 == m1, midx, float(pg)), axis=1, keepdims=True)
    m2 = jnp.max(jnp.where(midx == f1, _NEG_INF, sel3), axis=1, keepdims=True)
    grp = m1 + m2

    gsel = jnp.zeros((_N_GROUPS, 1, tm), _F32)
    for _ in range(_TOPK_GROUPS):
        m = jnp.max(grp, axis=0, keepdims=True)
        f = jnp.min(jnp.where(grp == m, gidx, float(_N_GROUPS)), axis=0, keepdims=True)
        hit = gidx == f
        gsel = jnp.where(hit, 1.0, gsel)
        grp = jnp.where(hit, _NEG_INF, grp)
    cur = jnp.where(gsel > 0.0, sel3, _NEG_INF)

    chosen = jnp.zeros(shp, _F32)
    firsts = []
    for _ in range(_TOP_K):
        m = jnp.max(jnp.max(cur, axis=1, keepdims=True), axis=0, keepdims=True)
        f = jnp.min(jnp.min(jnp.where(cur == m, eid3, float(n_exp)), axis=1, keepdims=True), axis=0, keepdims=True)
        hit = eid3 == f
        chosen = jnp.where(hit, 1.0, chosen)
        cur = jnp.where(hit, _NEG_INF, cur)
        firsts.append(f)

    wsel = scores3 * chosen
    wsum = jnp.sum(jnp.sum(wsel, axis=1, keepdims=True), axis=0, keepdims=True)
    gates3 = wsel / wsum * _ROUTED_SCALE

    ch2 = chosen.reshape(n_exp, tm)
    upper = (lax.broadcasted_iota(jnp.int32, (tm, tm), 0) < lax.broadcasted_iota(jnp.int32, (tm, tm), 1))
    prefix = jnp.dot(ch2.astype(_BF16), jnp.where(upper, 1.0, 0.0).astype(_BF16), preferred_element_type=_F32)
    carry = carry_sc[...]
    rank3 = (carry + prefix).reshape(shp)
    new_carry = carry + jnp.sum(ch2, axis=-1, keepdims=True)
    carry_sc[...] = new_carry
    cnt_ref[...] = new_carry

    for r in range(_TOP_K):
        hit = eid3 == firsts[r]
        rk = jnp.sum(jnp.sum(jnp.where(hit, rank3, 0.0), axis=1, keepdims=True), axis=0, keepdims=True)
        gt = jnp.sum(jnp.sum(jnp.where(hit, gates3, 0.0), axis=1, keepdims=True), axis=0, keepdims=True)
        eidx_ref[r:r + 1, :] = firsts[r].reshape(1, tm).astype(jnp.int32)
        rank_ref[r:r + 1, :] = rk.reshape(1, tm).astype(jnp.int32)
        gate_ref[r:r + 1, :] = gt.reshape(1, tm)


def _route(xa, na_rows, xb, g, modt, rows_per_mod, wt, bias, tm):
    d = xa.shape[1]
    n_exp = wt.shape[0]
    na = na_rows // tm
    nb = 0 if xb is None else xb.shape[0] // tm
    n = (na + nb) * tm
    tpm = rows_per_mod // tm
    n_mod = modt.shape[0]
    in_specs = [pl.BlockSpec((tm, d), lambda i: (jnp.minimum(i, na - 1), 0))]
    args = [xa]
    if xb is not None:
        in_specs.append(pl.BlockSpec((tm, d), lambda i: (jnp.maximum(i - na, 0), 0)))
        args.append(xb)
    in_specs += [pl.BlockSpec((1, d), lambda i: (0, 0)),
                 pl.BlockSpec((None, _N_MOD, d), lambda i: (jnp.minimum(i // tpm, n_mod - 1), 0, 0)),
                 pl.BlockSpec((n_exp, d), lambda i: (0, 0)),
                 pl.BlockSpec((n_exp, 1), lambda i: (0, 0))]
    args += [g, modt, wt, bias]
    return pl.pallas_call(
        functools.partial(_route_kernel, na=na, two_inputs=xb is not None, n_exp=n_exp),
        grid=(na + nb,),
        in_specs=in_specs,
        out_specs=[pl.BlockSpec((tm, d // 2), lambda i: (i, 0)),
                   pl.BlockSpec((_TOP_K, tm), lambda i: (0, i)),
                   pl.BlockSpec((_TOP_K, tm), lambda i: (0, i)),
                   pl.BlockSpec((_TOP_K, tm), lambda i: (0, i)),
                   pl.BlockSpec((n_exp, 1), lambda i: (0, 0))],
        out_shape=[jax.ShapeDtypeStruct((n, d // 2), _U32),
                   jax.ShapeDtypeStruct((_TOP_K, n), jnp.int32),
                   jax.ShapeDtypeStruct((_TOP_K, n), jnp.int32),
                   jax.ShapeDtypeStruct((_TOP_K, n), _F32),
                   jax.ShapeDtypeStruct((n_exp, 1), _F32)],
        scratch_shapes=[pltpu.VMEM((n_exp, 1), _F32)],
        compiler_params=_params(("arbitrary",)),
    )(*args)


def _dispatch_kernel(dest_ref, hp_ref, xs_ref, sem):
    tm = hp_ref.shape[0]

    def row_copy(n, k):
        return pltpu.make_async_copy(hp_ref.at[pl.ds(n, 1), :], xs_ref.at[pl.ds(dest_ref[k, n], 1), :], sem)

    def body(n, carry):
        for k in range(_TOP_K):
            row_copy(n, k).start()
        return carry

    lax.fori_loop(0, tm, body, 0, unroll=4)
    for _ in range(_TOP_K):
        pltpu.make_async_copy(hp_ref, xs_ref.at[pl.ds(0, tm), :], sem).wait()


def _dispatch(hp, dest8, tm):
    n, dh = hp.shape
    return pl.pallas_call(
        _dispatch_kernel,
        grid=(n // tm,),
        in_specs=[pl.BlockSpec((_TOP_K, tm), lambda i: (0, i), memory_space=pltpu.SMEM),
                  pl.BlockSpec((tm, dh), lambda i: (i, 0))],
        out_specs=pl.BlockSpec(memory_space=pl.ANY),
        out_shape=jax.ShapeDtypeStruct((n * _TOP_K, dh), _U32),
        scratch_shapes=[pltpu.SemaphoreType.DMA(())],
        compiler_params=_params(("arbitrary",), disable_bounds_checks=True),
    )(dest8, hp)


def _swiglu_packed(xu, w1, w3, w2):
    dh = xu.shape[1]
    x_lo = _unpack_lo(xu).astype(_BF16)
    x_hi = _unpack_hi(xu).astype(_BF16)
    h1 = (jnp.dot(x_lo, w1[:dh], preferred_element_type=_F32) + jnp.dot(x_hi, w1[dh:], preferred_element_type=_F32))
    h3 = (jnp.dot(x_lo, w3[:dh], preferred_element_type=_F32) + jnp.dot(x_hi, w3[dh:], preferred_element_type=_F32))
    a = (h1 * jax.nn.sigmoid(h1) * h3).astype(_BF16)
    return jnp.dot(a, w2, preferred_element_type=_F32)


def _expert_kernel(tile_ref, exp_ref, lo_ref, hi_ref, first_ref, newexp_ref, valid_ref,
                   x_ref, w1_ref, w3_ref, w2_ref, y_ref, w1b, w3b, w2b):
    v = pl.program_id(0)

    @pl.when(newexp_ref[v] == 1)
    def _():
        w1b[...] = w1_ref[...].astype(_BF16)
        w3b[...] = w3_ref[...].astype(_BF16)
        w2b[...] = w2_ref[...].astype(_BF16)

    @pl.when(valid_ref[v] == 1)
    def _():
        tmx, dh = x_ref.shape
        y = _swiglu_packed(x_ref[...], w1b[...], w3b[...], w2b[...])
        packed = _pack_bf16_pair(y[:, :dh], y[:, dh:])
        rows = lax.broadcasted_iota(jnp.int32, (tmx, 1), 0)
        mine = (rows >= lo_ref[v]) & (rows < hi_ref[v])

        @pl.when(first_ref[v] == 1)
        def _():
            y_ref[...] = jnp.where(mine, packed, jnp.zeros_like(packed))

        @pl.when(first_ref[v] == 0)
        def _():
            y_ref[...] = jnp.where(mine, packed, y_ref[...])


def _cumsum_small(v):
    n = v.shape[0]
    tri = jnp.arange(n)[:, None] >= jnp.arange(n)[None, :]
    return jnp.sum(jnp.where(tri, v[None, :], 0), axis=1).astype(jnp.int32)


def _take_small(table, idx):
    hot = idx[..., None] == jnp.arange(table.shape[0], dtype=jnp.int32)
    return jnp.sum(jnp.where(hot, table, 0), axis=-1).astype(jnp.int32)


def _expert_visits(counts, n_rows, tmx):
    n_exp = counts.shape[0]
    n_tiles = n_rows // tmx
    n_vis = n_tiles + n_exp - 1
    ends = _cumsum_small(counts)
    starts = ends - counts
    t_first = starts // tmx
    t_last = jnp.where(counts > 0, (ends - 1) // tmx, t_first - 1)
    nv = t_last - t_first + 1
    v_end = _cumsum_small(nv)
    v_start = v_end - nv
    total = v_end[-1]
    v = jnp.arange(n_vis, dtype=jnp.int32)
    vc = jnp.minimum(v, total - 1)
    e = jnp.sum((v_end[None, :] <= vc[:, None]).astype(jnp.int32), axis=1)
    tile = _take_small(t_first, e) + (vc - _take_small(v_start, e))
    lo = jnp.clip(_take_small(starts, e) - tile * tmx, 0, tmx)
    hi = jnp.clip(_take_small(ends, e) - tile * tmx, 0, tmx)
    valid = (v < total).astype(jnp.int32)
    prev_tile = jnp.concatenate([jnp.full((1,), -1, jnp.int32), tile[:-1]])
    prev_e = jnp.concatenate([jnp.full((1,), -1, jnp.int32), e[:-1]])
    first = ((tile != prev_tile) & (valid == 1)).astype(jnp.int32)
    newexp = ((e != prev_e) & (valid == 1)).astype(jnp.int32)
    return tile, e, lo, hi, first, newexp, valid


def _experts(xs, counts, w1, w3, w2, layer, tmx):
    p, dh = xs.shape
    _, n_exp, d, f = w1.shape
    meta = _expert_visits(counts, p, tmx)
    n_vis = meta[0].shape[0]
    grid_spec = pltpu.PrefetchScalarGridSpec(
        num_scalar_prefetch=7,
        grid=(n_vis,),
        in_specs=[pl.BlockSpec((tmx, dh), lambda v, t, e, *_: (t[v], 0)),
                  pl.BlockSpec((None, None, d, f), lambda v, t, e, *_: (layer, e[v], 0, 0)),
                  pl.BlockSpec((None, None, d, f), lambda v, t, e, *_: (layer, e[v], 0, 0)),
                  pl.BlockSpec((None, None, f, d), lambda v, t, e, *_: (layer, e[v], 0, 0))],
        out_specs=pl.BlockSpec((tmx, dh), lambda v, t, e, *_: (t[v], 0)),
        scratch_shapes=[pltpu.VMEM((d, f), _BF16), pltpu.VMEM((d, f), _BF16), pltpu.VMEM((f, d), _BF16)],
    )
    return pl.pallas_call(
        _expert_kernel,
        grid_spec=grid_spec,
        out_shape=jax.ShapeDtypeStruct((p, dh), _U32),
        compiler_params=_params(("arbitrary",)),
    )(*meta, xs, w1, w3, w2)


def _combine_kernel(*refs, na, n_steps, two_inputs, final_norm):
    dcur_ref, dnext_ref, w_ref, hp_ref, xa_ref = refs[:5]
    refs = refs[5:]
    xb_ref = None
    if two_inputs:
        xb_ref, refs = refs[0], refs[1:]
    mod_ref, s1_ref, s3_ref, s2_ref = refs[:4]
    refs = refs[4:]
    fg_ref = None
    if final_norm:
        fg_ref, refs = refs[0], refs[1:]
    ys_ref, o_ref, gbuf, sems, ysh_sc = refs
    i = pl.program_id(0)
    slot = i % 2
    tm, dh = hp_ref.shape

    def row_copy(dref, sl, n, k):
        return pltpu.make_async_copy(ys_ref.at[pl.ds(dref[k, n], 1), :], gbuf.at[sl, k, pl.ds(n, 1), :], sems.at[sl])

    def issue(dref, sl):
        def body(n, carry):
            for k in range(_TOP_K):
                row_copy(dref, sl, n, k).start()
            return carry

        lax.fori_loop(0, tm, body, 0)

    def wait_slot(sl):
        for k in range(_TOP_K):
            pltpu.make_async_copy(ys_ref.at[pl.ds(0, tm), :], gbuf.at[sl, k], sems.at[sl]).wait()

    @pl.when(i == 0)
    def _():
        issue(dcur_ref, 0)

    ysh_sc[...] = _swiglu_packed(hp_ref[...], s1_ref[...], s3_ref[...], s2_ref[...])
    wait_slot(slot)

    def rows_body(r, carry):
        for t in range(8):
            for k in range(_TOP_K):
                row_copy(dnext_ref, 1 - slot, r * 8 + t, k).start()
        rows = pl.ds(pl.multiple_of(r * 8, 8), 8)
        acc_lo = ysh_sc[rows, :dh]
        acc_hi = ysh_sc[rows, dh:]
        for k in range(_TOP_K):
            u = gbuf[slot, k, rows, :]
            wk = w_ref[rows, k:k + 1]
            acc_lo = acc_lo + wk * _unpack_lo(u)
            acc_hi = acc_hi + wk * _unpack_hi(u)
        x = xa_ref[rows, :]
        if two_inputs:
            x = jnp.where(i < na, x, xb_ref[rows, :])
        out_lo = x[:, :dh] + mod_ref[5:6, :dh] * acc_lo
        out_hi = x[:, dh:] + mod_ref[5:6, dh:] * acc_hi
        if final_norm:
            ms = (jnp.sum(out_lo * out_lo, axis=-1, keepdims=True)
                  + jnp.sum(out_hi * out_hi, axis=-1, keepdims=True)) / (2 * dh)
            rs = lax.rsqrt(ms + _NORM_EPS)
            out_lo = out_lo * rs * fg_ref[:, :dh]
            out_hi = out_hi * rs * fg_ref[:, dh:]
        o_ref[rows, :dh] = out_lo
        o_ref[rows, dh:] = out_hi
        return carry

    lax.fori_loop(0, tm // 8, rows_body, 0)

    @pl.when(i == n_steps - 1)
    def _():
        wait_slot(1 - slot)


def _combine(ys, dest8, w8t, hp, xa, na_rows, xb, modt, rows_per_mod, s1, s3, s2, final_g, tm):
    n, dh = hp.shape
    d = 2 * dh
    f = s1.shape[1]
    n_steps = n // tm
    na = na_rows // tm
    tpm = rows_per_mod // tm
    n_mod = modt.shape[0]
    smem_spec = lambda fn: pl.BlockSpec((_TOP_K, tm), fn, memory_space=pltpu.SMEM)
    in_specs = [smem_spec(lambda i: (0, i)),
                smem_spec(lambda i: (0, jnp.minimum(i + 1, n_steps - 1))),
                pl.BlockSpec((tm, _TOP_K), lambda i: (i, 0)),
                pl.BlockSpec((tm, dh), lambda i: (i, 0)),
                pl.BlockSpec((tm, d), lambda i: (jnp.minimum(i, na - 1), 0))]
    args = [dest8, dest8, w8t, hp, xa]
    if xb is not None:
        in_specs.append(pl.BlockSpec((tm, d), lambda i: (jnp.maximum(i - na, 0), 0)))
        args.append(xb)
    in_specs += [pl.BlockSpec((None, _N_MOD, d), lambda i: (jnp.minimum(i // tpm, n_mod - 1), 0, 0)),
                 pl.BlockSpec((d, f), lambda i: (0, 0)),
                 pl.BlockSpec((d, f), lambda i: (0, 0)),
                 pl.BlockSpec((f, d), lambda i: (0, 0))]
    args += [modt, s1, s3, s2]
    if final_g is not None:
        in_specs.append(pl.BlockSpec((1, d), lambda i: (0, 0)))
        args.append(final_g)
    in_specs.append(pl.BlockSpec(memory_space=pl.ANY))
    args.append(ys)
    return pl.pallas_call(
        functools.partial(_combine_kernel, na=na, n_steps=n_steps, two_inputs=xb is not None,
                          final_norm=final_g is not None),
        grid=(n_steps,),
        in_specs=in_specs,
        out_specs=pl.BlockSpec((tm, d), lambda i: (i, 0)),
        out_shape=jax.ShapeDtypeStruct((n, d), _F32),
        scratch_shapes=[pltpu.VMEM((2, _TOP_K, tm, dh), _U32), pltpu.SemaphoreType.DMA((2,)),
                        pltpu.VMEM((tm, d), _F32)],
        compiler_params=_params(("arbitrary",), disable_bounds_checks=True),
    )(*args)


def _moe(xa, na_rows, xb, g, modt, rows_per_mod, router_w, router_bias, w1, w3, w2, layer, s1, s3, s2, final_g):
    hp, eidx8, rank8, gate8, counts = _route(xa, na_rows, xb, g, modt, rows_per_mod,
                                             router_w.T, router_bias.reshape(-1, 1), tm=256)
    counts = counts[:, 0].astype(jnp.int32)
    starts = _cumsum_small(counts) - counts
    dest8 = _take_small(starts, eidx8) + rank8
    xs = _dispatch(hp, dest8, tm=256)
    ys = _experts(xs, counts, w1, w3, w2, layer, tmx=256)
    return _combine(ys, dest8, gate8.T, hp, xa, na_rows, xb, modt, rows_per_mod,
                    s1.astype(_BF16), s3.astype(_BF16), s2.astype(_BF16), final_g, tm=128)


def kernel(x, c, ctx, c_ctx, ada_w, ada_b, norm_g, fourier_wo, fourier_bo, da_wqkv, da_wo, da_lambda, da_subln_g, router_w, router_bias, exp_w1, exp_w3, exp_w2, shared_w1, shared_w3, shared_w2, final_g):
    b, s, d = x.shape
    n_ctx = ctx.shape[1]
    depth = ada_w.shape[0]
    assert depth == 2 and b + 1 <= 8 and s % _GRID_W == 0
    assert s % 2048 == 0 and d % 1024 == 0 and n_ctx % 128 == 0 and d == _V_DIM * (d // _V_DIM)
    nl, nc = b * s, b * n_ctx

    cond8 = jnp.zeros((8, d), _F32).at[:b].set(c).at[b].set(c_ctx)
    mods = _adaln(cond8, ada_w, ada_b).reshape(depth, 8, _N_MOD, d)

    modt = mods[0, :b + 1]
    g_mix, g_moe = norm_g[0, 0:1], norm_g[0, 1:2]
    n2 = 64
    n1 = s // n2
    chan, m1, m2 = _dft_tables(s, n1, n2, d // _FOURIER_GROUPS)
    wo = fourier_wo[0].astype(_BF16)
    bo = fourier_bo[0:1]

    yl = _normmod_chandft(x, g_mix, modt, True, chan, tm=256)
    fl = _seq_dft(yl, m1, m2, n1, n2, tn=256)
    xl = _mm(fl.reshape(nl, d), wo, n_out=d, bias=bo, res=x.reshape(nl, d), gate=modt[:, 2:3], rows_per_gate=s,
             tm=1024, tn=512)

    yc = _normmod_chandft(ctx, g_mix, modt[b:b + 1], False, chan, tm=n_ctx)
    fc = _ctx_seq_dft(yc, tn=512)
    xc = _mm(fc.reshape(nc, d), wo, n_out=d, bias=bo, res=ctx.reshape(nc, d), gate=modt[b:b + 1, 2:3],
             rows_per_gate=nc, tm=n_ctx, tn=512)

    x_all = _moe(xl, nl, xc, g_moe, modt, s, router_w[0], router_bias[0], exp_w1, exp_w3, exp_w2, 0,
                 shared_w1[0], shared_w3[0], shared_w2[0], None)

    modt = mods[1, :b + 1]
    g_mix, g_moe = norm_g[1, 0:1], norm_g[1, 1:2]
    lam_init = 0.8 - 0.6 * float(np.exp(-0.3 * 1))
    wqkv = _rope_friendly_qkv_weights(da_wqkv[0], tn=512)
    cos, sin = _rope_tables(s)

    hl = _normmod(x_all, 0, nl, g_mix, modt, s, 0, tm=512)
    hc = _normmod(x_all, nl, nc, g_mix, modt, nc, b, tm=n_ctx)
    qk, vt = _qkv(hl, wqkv, cos, sin, s, tm=1024, tn=512)
    kvc = _mm(hc, wqkv, n_out=2 * d, w_col0=d, out_dtype=_BF16, tm=n_ctx, tn=512)
    o = _attention(qk.reshape(b, s, 2 * d), vt, kvc.reshape(b, n_ctx, 2 * d), kvc[:, d:].T, da_lambda[0],
                   da_subln_g[0], lam_init, tq=2048, tk=1024)
    xl = _mm(o.reshape(nl, d), da_wo[0].astype(_BF16), n_out=d, res=x_all, gate=modt[:, 2:3], rows_per_gate=s,
             tm=1024, tn=512)

    out = _moe(xl, nl, None, g_moe, modt, s, router_w[1], router_bias[1], exp_w1, exp_w3, exp_w2, 1,
               shared_w1[1], shared_w3[1], shared_w2[1], final_g.reshape(1, d))
    return out.reshape(b, s, d)
```

```python
import functools
import math

import numpy as np
import jax
import jax.numpy as jnp
from jax import lax
from jax.experimental import pallas as pl
from jax.experimental.pallas import tpu as pltpu

_GRID_W = 64
_N_MOD = 6
_NORM_EPS = 1e-6
_FOURIER_GROUPS = 8
_HEAD_DIM = 128
_V_DIM = 2 * _HEAD_DIM
_ROPE_BASE = 10000.0
_TOP_K = 8
_N_GROUPS = 8
_TOPK_GROUPS = 4
_ROUTED_SCALE = 2.5

_LANE = 128
_VMEM_LIMIT = 56 * 1024 * 1024

_F32 = jnp.float32
_BF16 = jnp.bfloat16
_U32 = jnp.uint32
_HI_MASK = np.uint32(0xFFFF0000)
_NEG_INF = float("-inf")


def _params(sem, **kw):
    return pltpu.CompilerParams(dimension_semantics=sem, vmem_limit_bytes=_VMEM_LIMIT, **kw)


def _pack_bf16_pair(lo, hi):
    ul = pltpu.bitcast(lo.astype(_BF16).astype(_F32), _U32)
    uh = pltpu.bitcast(hi.astype(_BF16).astype(_F32), _U32)
    return (ul >> 16) | (uh & _HI_MASK)


def _unpack_lo(u):
    return pltpu.bitcast(u << 16, _F32)


def _unpack_hi(u):
    return pltpu.bitcast(u & _HI_MASK, _F32)


def _rms_mod(x, g, shift, scale):
    ms = jnp.mean(x * x, axis=-1, keepdims=True)
    return x * lax.rsqrt(ms + _NORM_EPS) * g * (1.0 + scale) + shift


def _adaln_kernel(c_ref, w_ref, b_ref, o_ref):
    c = c_ref[...]
    s = (c * jax.nn.sigmoid(c)).astype(_BF16)
    o_ref[...] = jnp.dot(s, w_ref[...].astype(_BF16), preferred_element_type=_F32) + b_ref[...]


def _adaln(cond8, ada_w, ada_b):
    depth, d, n6 = ada_w.shape
    tn = min(512, n6)
    return pl.pallas_call(
        _adaln_kernel,
        grid=(depth, n6 // tn),
        in_specs=[pl.BlockSpec((8, d), lambda l, j: (0, 0)),
                  pl.BlockSpec((None, d, tn), lambda l, j: (l, 0, j)),
                  pl.BlockSpec((None, 1, tn), lambda l, j: (l, 0, j))],
        out_specs=pl.BlockSpec((None, 8, tn), lambda l, j: (l, 0, j)),
        out_shape=jax.ShapeDtypeStruct((depth, 8, n6), _F32),
        compiler_params=_params(("arbitrary", "arbitrary")),
    )(cond8, ada_w, ada_b.reshape(depth, 1, n6))


def _normmod_chandft_kernel(x_ref, g_ref, mod_ref, tab_ref, o_ref, *, groups):
    h = _rms_mod(x_ref[...], g_ref[...], mod_ref[0:1, :], mod_ref[1:2, :]).astype(_BF16)
    cg = h.shape[-1] // groups
    for g in range(groups):
        r = jnp.dot(h[:, g * cg:(g + 1) * cg], tab_ref[...], preferred_element_type=_F32)
        o_ref[:, g * cg:(g + 1) * cg] = _pack_bf16_pair(r[:, :cg], r[:, cg:])


def _normmod_chandft(x3, g, mod, mod_per_batch, tab, tm):
    b, t, d = x3.shape
    mod_map = (lambda bi, i: (bi, 0, 0)) if mod_per_batch else (lambda bi, i: (0, 0, 0))
    return pl.pallas_call(
        functools.partial(_normmod_chandft_kernel, groups=_FOURIER_GROUPS),
        grid=(b, t // tm),
        in_specs=[pl.BlockSpec((None, tm, d), lambda bi, i: (bi, i, 0)),
                  pl.BlockSpec((1, d), lambda bi, i: (0, 0)),
                  pl.BlockSpec((None, _N_MOD, d), mod_map),
                  pl.BlockSpec(tab.shape, lambda bi, i: (0, 0))],
        out_specs=pl.BlockSpec((None, tm, d), lambda bi, i: (bi, i, 0)),
        out_shape=jax.ShapeDtypeStruct((b, t, d), _U32),
        compiler_params=_params(("arbitrary", "arbitrary")),
    )(x3, g, mod, tab)


def _normmod_kernel(x_ref, g_ref, mod_ref, o_ref):
    o_ref[...] = _rms_mod(x_ref[...], g_ref[...], mod_ref[0:1, :], mod_ref[1:2, :]).astype(_BF16)


def _normmod(x2, row0, rows, g, mod, rows_per_mod, mod0, tm):
    d = x2.shape[1]
    t0 = row0 // tm
    tpm = rows_per_mod // tm
    return pl.pallas_call(
        _normmod_kernel,
        grid=(rows // tm,),
        in_specs=[pl.BlockSpec((tm, d), lambda i: (t0 + i, 0)),
                  pl.BlockSpec((1, d), lambda i: (0, 0)),
                  pl.BlockSpec((None, _N_MOD, d), lambda i: (mod0 + i // tpm, 0, 0))],
        out_specs=pl.BlockSpec((tm, d), lambda i: (i, 0)),
        out_shape=jax.ShapeDtypeStruct((rows, d), _BF16),
        compiler_params=_params(("arbitrary",)),
    )(x2, g, mod)


def _unpack_complex_rows(u):
    return jnp.concatenate([_unpack_lo(u), _unpack_hi(u)], axis=0).astype(_BF16)


def _seq_dft_kernel(*refs, n1, n2, n_col):
    y_refs = refs[:n_col]
    m1_ref, m2_ref, o_ref, a_sc, f_sc = refs[n_col:]

    def stage1(t1, carry):
        u = jnp.concatenate([y[pl.ds(t1, n2, stride=n1), :] for y in y_refs], axis=1)
        r = jnp.dot(m1_ref[t1], _unpack_complex_rows(u), preferred_element_type=_F32)
        packed = _pack_bf16_pair(r[:n2], r[n2:])
        for c in range(n_col):
            a_sc[c, pl.ds(t1, n2, stride=n1), :] = packed[:, c * _LANE:(c + 1) * _LANE]
        return carry

    lax.fori_loop(0, n1, stage1, 0, unroll=8)

    def stage2(k2, carry):
        rows = pl.ds(pl.multiple_of(k2 * n1, n1), n1)
        u = jnp.concatenate([a_sc[c, rows, :] for c in range(n_col)], axis=1)
        r = jnp.dot(m2_ref[...], _unpack_complex_rows(u), preferred_element_type=_F32)
        for c in range(n_col):
            f_sc[c, pl.ds(k2, n1, stride=n2), :] = r[:, c * _LANE:(c + 1) * _LANE]
        return carry

    lax.fori_loop(0, n2, stage2, 0, unroll=8)
    for c in range(n_col):
        o_ref[:, c * _LANE:(c + 1) * _LANE] = f_sc[c].astype(_BF16)


def _cos_sin(n, rows, cols):
    ang = 2.0 * np.pi * ((np.arange(rows, dtype=np.int64)[:, None] * np.arange(cols, dtype=np.int64)[None, :]) % n) / n
    return np.cos(ang), np.sin(ang)


def _dft_tables(t, n1, n2, cg):
    cc, sc = _cos_sin(cg, cg, cg)
    chan = np.concatenate([cc, -sc], axis=1) / math.sqrt(cg)
    k2 = np.arange(n2, dtype=np.int64)
    idx = (k2[None, :, None] * k2[None, None, :] * n1 + k2[None, :, None] * np.arange(n1, dtype=np.int64)[:, None, None]) % t
    gr = np.cos(2.0 * np.pi * idx / t) / math.sqrt(t)
    gi = -np.sin(2.0 * np.pi * idx / t) / math.sqrt(t)
    m1 = np.concatenate([np.concatenate([gr, -gi], axis=2), np.concatenate([gi, gr], axis=2)], axis=1)
    c1, s1 = _cos_sin(n1, n1, n1)
    m2 = np.concatenate([c1, s1], axis=1)
    return jnp.asarray(chan, _BF16), jnp.asarray(m1, _BF16), jnp.asarray(m2, _BF16)


def _seq_dft(y, m1, m2, n1, n2, tn):
    b, t, d = y.shape
    n_col = tn // _LANE
    col_spec = lambda c: pl.BlockSpec((None, t, _LANE), lambda bi, j: (bi, 0, n_col * j + c))
    return pl.pallas_call(
        functools.partial(_seq_dft_kernel, n1=n1, n2=n2, n_col=n_col),
        grid=(b, d // tn),
        in_specs=[col_spec(c) for c in range(n_col)]
        + [pl.BlockSpec(m1.shape, lambda bi, j: (0, 0, 0)),
           pl.BlockSpec(m2.shape, lambda bi, j: (0, 0))],
        out_specs=pl.BlockSpec((None, t, tn), lambda bi, j: (bi, 0, j)),
        out_shape=jax.ShapeDtypeStruct((b, t, d), _BF16),
        scratch_shapes=[pltpu.VMEM((n_col, t, _LANE), _U32), pltpu.VMEM((n_col, t, _LANE), _F32)],
        compiler_params=_params(("arbitrary", "arbitrary")),
    )(*([y] * n_col), m1, m2)


def _ctx_dft_kernel(tab_ref, y_ref, o_ref):
    o_ref[...] = jnp.dot(tab_ref[...], _unpack_complex_rows(y_ref[...]), preferred_element_type=_F32).astype(_BF16)


def _ctx_seq_dft(y, tn):
    b, t, d = y.shape
    c, s = _cos_sin(t, t, t)
    tab = jnp.asarray(np.concatenate([c, s], axis=1) / math.sqrt(t), _BF16)
    return pl.pallas_call(
        _ctx_dft_kernel,
        grid=(b, d // tn),
        in_specs=[pl.BlockSpec(tab.shape, lambda bi, j: (0, 0)),
                  pl.BlockSpec((None, t, tn), lambda bi, j: (bi, 0, j))],
        out_specs=pl.BlockSpec((None, t, tn), lambda bi, j: (bi, 0, j)),
        out_shape=jax.ShapeDtypeStruct((b, t, d), _BF16),
        compiler_params=_params(("arbitrary", "arbitrary")),
    )(tab, y)


def _mm_kernel(*refs, has_bias, has_res):
    a_ref, w_ref = refs[0], refs[1]
    o_ref = refs[-1]
    acc = jnp.dot(a_ref[...], w_ref[...], preferred_element_type=_F32)
    k = 2
    if has_bias:
        acc = acc + refs[k][...]
        k += 1
    if has_res:
        acc = refs[k][...] + refs[k + 1][...] * acc
    o_ref[...] = acc.astype(o_ref.dtype)


def _mm(a, w, *, n_out, w_col0=0, bias=None, res=None, res_row0=0, gate=None, rows_per_gate=None, gate0=0,
        out_dtype=_F32, tm, tn):
    m, k = a.shape
    wj0 = w_col0 // tn
    in_specs = [pl.BlockSpec((tm, k), lambda i, j: (i, 0)),
                pl.BlockSpec((k, tn), lambda i, j: (0, wj0 + j))]
    args = [a, w]
    if bias is not None:
        in_specs.append(pl.BlockSpec((1, tn), lambda i, j: (0, j)))
        args.append(bias)
    if res is not None:
        r0 = res_row0 // tm
        tpg = rows_per_gate // tm
        in_specs.append(pl.BlockSpec((tm, tn), lambda i, j: (r0 + i, j)))
        in_specs.append(pl.BlockSpec((None, 1, tn), lambda i, j: (gate0 + i // tpg, 0, j)))
        args += [res, gate]
    return pl.pallas_call(
        functools.partial(_mm_kernel, has_bias=bias is not None, has_res=res is not None),
        grid=(m // tm, n_out // tn),
        in_specs=in_specs,
        out_specs=pl.BlockSpec((tm, tn), lambda i, j: (i, j)),
        out_shape=jax.ShapeDtypeStruct((m, n_out), out_dtype),
        compiler_params=_params(("arbitrary", "arbitrary")),
    )(*args)


def _qkv_kernel(a_ref, w_ref, cos_ref, sin_ref, o_ref, vt_ref, *, nq, nk, scale):
    j = pl.program_id(1)
    acc = jnp.dot(a_ref[...], w_ref[...], preferred_element_type=_F32)
    tn = acc.shape[1]

    @pl.when(j >= nq + nk)
    def _():
        vt_ref[...] = acc.T.astype(_BF16)

    @pl.when(j < nq + nk)
    def _():
        cos = cos_ref[...]
        sin = sin_ref[...]
        sc = jnp.where(j < nq, scale, 1.0).astype(_F32)
        for u in range(tn // _HEAD_DIM):
            x = acc[:, u * _HEAD_DIM:(u + 1) * _HEAD_DIM]
            sw = pltpu.roll(x, _HEAD_DIM // 2, 1)
            o_ref[:, u * _HEAD_DIM:(u + 1) * _HEAD_DIM] = ((x * cos + sw * sin) * sc).astype(_BF16)


def _rope_tables(s):
    n_freq = _HEAD_DIM // 4
    freqs = _ROPE_BASE ** (-jnp.arange(n_freq, dtype=_F32) / n_freq)
    pos = jnp.arange(s, dtype=jnp.int32)
    ang_r = (pos // _GRID_W).astype(_F32)[:, None] * freqs[None, :]
    ang_c = (pos % _GRID_W).astype(_F32)[:, None] * freqs[None, :]
    cos = jnp.concatenate([jnp.cos(ang_r), jnp.cos(ang_c)] * 2, axis=1)
    sin = jnp.concatenate([-jnp.sin(ang_r), -jnp.sin(ang_c), jnp.sin(ang_r), jnp.sin(ang_c)], axis=1)
    return cos, sin


def _prep_wqkv_kernel(w_ref, p_ref, o_ref, *, nqk):
    w = w_ref[...].astype(_BF16)

    @pl.when(pl.program_id(0) < nqk)
    def _():
        o_ref[...] = jnp.dot(w, p_ref[...], preferred_element_type=_F32).astype(_BF16)

    @pl.when(pl.program_id(0) >= nqk)
    def _():
        o_ref[...] = w


def _rope_friendly_qkv_weights(wqkv, tn):
    d = wqkv.shape[0]
    q4 = _HEAD_DIM // 4
    src = np.arange(tn).reshape(tn // _HEAD_DIM, 2, 2, q4).swapaxes(1, 2).reshape(tn)
    perm = np.zeros((tn, tn), np.float32)
    perm[src, np.arange(tn)] = 1.0
    return pl.pallas_call(
        functools.partial(_prep_wqkv_kernel, nqk=2 * d // tn),
        grid=(3 * d // tn,),
        in_specs=[pl.BlockSpec((d, tn), lambda j: (0, j)),
                  pl.BlockSpec((tn, tn), lambda j: (0, 0))],
        out_specs=pl.BlockSpec((d, tn), lambda j: (0, j)),
        out_shape=jax.ShapeDtypeStruct((d, 3 * d), _BF16),
        compiler_params=_params(("arbitrary",)),
    )(wqkv, jnp.asarray(perm, _BF16))


def _qkv(a, w, cos, sin, s, tm, tn):
    m, k = a.shape
    d = w.shape[1] // 3
    tps = s // tm
    nqk = 2 * d // tn
    return pl.pallas_call(
        functools.partial(_qkv_kernel, nq=d // tn, nk=d // tn, scale=_HEAD_DIM ** -0.5 * math.log2(math.e)),
        grid=(m // tm, 3 * d // tn),
        in_specs=[pl.BlockSpec((tm, k), lambda i, j: (i, 0)),
                  pl.BlockSpec((k, tn), lambda i, j: (0, j)),
                  pl.BlockSpec((tm, _HEAD_DIM), lambda i, j: (i % tps, 0)),
                  pl.BlockSpec((tm, _HEAD_DIM), lambda i, j: (i % tps, 0))],
        out_specs=[pl.BlockSpec((tm, tn), lambda i, j: (i, jnp.minimum(j, nqk - 1))),
                   pl.BlockSpec((tn, tm), lambda i, j: (jnp.maximum(j - nqk, 0), i))],
        out_shape=[jax.ShapeDtypeStruct((m, 2 * d), _BF16), jax.ShapeDtypeStruct((d, m), _BF16)],
        compiler_params=_params(("arbitrary", "arbitrary")),
    )(a, w, cos, sin)


def _attn_kernel(q_ref, k_ref, vt_ref, kc_ref, vct_ref, lam_ref, g_ref, o_ref,
                 s0, s1, p0, p1, m0, m1, a0, a1, x0, x1, acc0, acc1, *, tk, n_chunks, lam_init):
    s_sc, p_sc, m_sc, a_sc, x_sc, acc_sc = (s0, s1), (p0, p1), (m0, m1), (a0, a1), (x0, x1), (acc0, acc1)
    q = q_ref[...]
    qs = (q[:, :_HEAD_DIM], q[:, _HEAD_DIM:])
    dn = (((1,), (1,)), ((), ()))

    def scores(kb, c):
        return lax.dot_general(kb[:, c * _HEAD_DIM:(c + 1) * _HEAD_DIM], qs[c], dn, preferred_element_type=_F32)

    def scores_to_scratch(kb, c):
        st = scores(kb, c)
        s_sc[c][...] = st
        x_sc[c][...] = jnp.max(st, axis=0, keepdims=True)

    def chunk(i):
        return pl.ds(i * tk, tk) if isinstance(i, int) else pl.ds(pl.multiple_of(i * tk, tk), tk)

    def k_chunk(i):
        return k_ref[chunk(i), :]

    def vt_chunk(i):
        return vt_ref[:, chunk(i)]

    dv = vt_ref.shape[0]

    def with_ones(vtb):
        return jnp.concatenate([vtb, jnp.ones((16, vtb.shape[1]), _BF16)], axis=0)

    kcb = kc_ref[...]
    vctb = with_ones(vct_ref[...])
    st_ctx = [scores(kcb, c) for c in range(2)]
    scores_to_scratch(k_chunk(0), 0)
    for c in range(2):
        m = jnp.max(st_ctx[c], axis=0, keepdims=True)
        p = jnp.exp2(st_ctx[c] - m)
        m_sc[c][...] = m
        acc_sc[c][...] = jnp.dot(vctb, p.astype(_BF16), preferred_element_type=_F32)
        a_sc[c][...] = jnp.ones(a_sc[c].shape, _F32)
    p_sc[1][...] = jnp.zeros(p_sc[1].shape, _BF16)

    def softmax(c):
        m_old = m_sc[c][...]
        m_new = jnp.maximum(m_old, x_sc[c][...])
        a_sc[c][...] = jnp.exp2(m_old - m_new)
        m_sc[c][...] = m_new
        p_sc[c][...] = jnp.exp2(s_sc[c][...] - m_new).astype(_BF16)

    def pv(c, i):
        acc_sc[c][...] = (a_sc[c][...] * acc_sc[c][...]
                          + jnp.dot(with_ones(vt_chunk(i)), p_sc[c][...], preferred_element_type=_F32))

    def body(i, carry):
        scores_to_scratch(k_chunk(i), 1)
        softmax(0)
        pv(1, jnp.maximum(i - 1, 0))
        scores_to_scratch(k_chunk(i + 1), 0)
        softmax(1)
        pv(0, i)
        return carry

    lax.fori_loop(0, n_chunks - 1, body, 0)
    last = n_chunks - 1
    scores_to_scratch(k_chunk(last), 1)
    softmax(0)
    pv(1, max(last - 1, 0))
    softmax(1)
    pv(0, last)
    pv(1, last)

    lp = lam_ref[...]
    lam = (jnp.exp(jnp.sum(lp[0:1] * lp[1:2], axis=-1, keepdims=True))
           - jnp.exp(jnp.sum(lp[2:3] * lp[3:4], axis=-1, keepdims=True)) + lam_init)
    ot = acc0[:dv, :] * (1.0 / acc0[dv:dv + 1, :]) - acc1[:dv, :] * (lam / acc1[dv:dv + 1, :])
    ms = jnp.mean(ot * ot, axis=0, keepdims=True)
    ot = ot * lax.rsqrt(ms + _NORM_EPS) * g_ref[...] * (1.0 - lam_init)
    o_ref[...] = ot.T.astype(_BF16)


def _attention(qk, vt, kvc, vct, lam_p, subln_g, lam_init, tq, tk):
    b, s, d2 = qk.shape
    d = d2 // 2
    h = d // _V_DIM
    c = kvc.shape[1]
    return pl.pallas_call(
        functools.partial(_attn_kernel, tk=tk, n_chunks=s // tk, lam_init=lam_init),
        grid=(b, h, s // tq),
        in_specs=[pl.BlockSpec((None, tq, _V_DIM), lambda bi, hi, qi: (bi, qi, hi)),
                  pl.BlockSpec((None, s, _V_DIM), lambda bi, hi, qi: (bi, 0, h + hi), pipeline_mode=pl.Buffered(1)),
                  pl.BlockSpec((_V_DIM, s), lambda bi, hi, qi: (hi, bi), pipeline_mode=pl.Buffered(1)),
                  pl.BlockSpec((None, c, _V_DIM), lambda bi, hi, qi: (bi, 0, hi)),
                  pl.BlockSpec((_V_DIM, c), lambda bi, hi, qi: (hi, bi)),
                  pl.BlockSpec(lam_p.shape, lambda bi, hi, qi: (0, 0)),
                  pl.BlockSpec((_V_DIM, 1), lambda bi, hi, qi: (0, 0))],
        out_specs=pl.BlockSpec((None, tq, _V_DIM), lambda bi, hi, qi: (bi, qi, hi)),
        out_shape=jax.ShapeDtypeStruct((b, s, d), _BF16),
        scratch_shapes=([pltpu.VMEM((tk, tq), _F32)] * 2 + [pltpu.VMEM((tk, tq), _BF16)] * 2
                        + [pltpu.VMEM((1, tq), _F32)] * 6 + [pltpu.VMEM((_V_DIM + 16, tq), _F32)] * 2),
        compiler_params=_params(("arbitrary", "arbitrary", "arbitrary")),
    )(qk, qk, vt, kvc, vct, lam_p, subln_g.reshape(_V_DIM, 1))


def _route_kernel(*refs, na, two_inputs, n_exp):
    if two_inputs:
        xa_ref, xb_ref = refs[0], refs[1]
        refs = refs[2:]
    else:
        xa_ref, xb_ref = refs[0], None
        refs = refs[1:]
    g_ref, mod_ref, wt_ref, bias_ref, hp_ref, eidx_ref, rank_ref, gate_ref, cnt_ref, carry_sc = refs
    i = pl.program_id(0)

    @pl.when(i == 0)
    def _():
        carry_sc[...] = jnp.zeros(carry_sc.shape, _F32)

    x = xa_ref[...]
    if two_inputs:
        x = jnp.where(i < na, x, xb_ref[...])
    tm, d = x.shape
    h = _rms_mod(x, g_ref[...], mod_ref[3:4, :], mod_ref[4:5, :])
    hp_ref[...] = _pack_bf16_pair(h[:, :d // 2], h[:, d // 2:])

    h_hi = h.astype(_BF16)
    h_lo = (h - h_hi.astype(_F32)).astype(_BF16)
    w = wt_ref[...]
    w_hi = w.astype(_BF16)
    w_lo = (w - w_hi.astype(_F32)).astype(_BF16)
    dn = (((1,), (1,)), ((), ()))
    logits = (lax.dot_general(w_hi, h_hi, dn, preferred_element_type=_F32)
              + lax.dot_general(w_lo, h_hi, dn, preferred_element_type=_F32)
              + lax.dot_general(w_hi, h_lo, dn, preferred_element_type=_F32))
    scores = jax.nn.sigmoid(logits)
    sel = scores + bias_ref[...]

    pg = n_exp // _N_GROUPS
    shp = (_N_GROUPS, pg, tm)
    sel3 = sel.reshape(shp)
    scores3 = scores.reshape(shp)
    midx = lax.broadcasted_iota(jnp.int32, shp, 1).astype(_F32)
    gidx = lax.broadcasted_iota(jnp.int32, (_N_GROUPS, 1, tm), 0).astype(_F32)
    eid3 = lax.broadcasted_iota(jnp.int32, shp, 0).astype(_F32) * pg + midx

    m1 = jnp.max(sel3, axis=1, keepdims=True)
    f1 = jnp.min(jnp.where(sel3 == m1, midx, float(pg)), axis=1, keepdims=True)
    m2 = jnp.max(jnp.where(midx == f1, _NEG_INF, sel3), axis=1, keepdims=True)
    grp = m1 + m2

    gsel = jnp.zeros((_N_GROUPS, 1, tm), _F32)
    for _ in range(_TOPK_GROUPS):
        m = jnp.max(grp, axis=0, keepdims=True)
        f = jnp.min(jnp.where(grp == m, gidx, float(_N_GROUPS)), axis=0, keepdims=True)
        hit = gidx == f
        gsel = jnp.where(hit, 1.0, gsel)
        grp = jnp.where(hit, _NEG_INF, grp)
    cur = jnp.where(gsel > 0.0, sel3, _NEG_INF)

    chosen = jnp.zeros(shp, _F32)
    firsts = []
    for _ in range(_TOP_K):
        m = jnp.max(jnp.max(cur, axis=1, keepdims=True), axis=0, keepdims=True)
        f = jnp.min(jnp.min(jnp.where(cur == m, eid3, float(n_exp)), axis=1, keepdims=True), axis=0, keepdims=True)
        hit = eid3 == f
        chosen = jnp.where(hit, 1.0, chosen)
        cur = jnp.where(hit, _NEG_INF, cur)
        firsts.append(f)

    wsel = scores3 * chosen
    wsum = jnp.sum(jnp.sum(wsel, axis=1, keepdims=True), axis=0, keepdims=True)
    gates3 = wsel / wsum * _ROUTED_SCALE

    ch2 = chosen.reshape(n_exp, tm)
    upper = (lax.broadcasted_iota(jnp.int32, (tm, tm), 0) < lax.broadcasted_iota(jnp.int32, (tm, tm), 1))
    prefix = jnp.dot(ch2.astype(_BF16), jnp.where(upper, 1.0, 0.0).astype(_BF16), preferred_element_type=_F32)
    carry = carry_sc[...]
    rank3 = (carry + prefix).reshape(shp)
    new_carry = carry + jnp.sum(ch2, axis=-1, keepdims=True)
    carry_sc[...] = new_carry
    cnt_ref[...] = new_carry

    for r in range(_TOP_K):
        hit = eid3 == firsts[r]
        rk = jnp.sum(jnp.sum(jnp.where(hit, rank3, 0.0), axis=1, keepdims=True), axis=0, keepdims=True)
        gt = jnp.sum(jnp.sum(jnp.where(hit, gates3, 0.0), axis=1, keepdims=True), axis=0, keepdims=True)
        eidx_ref[r:r + 1, :] = firsts[r].reshape(1, tm).astype(jnp.int32)
        rank_ref[r:r + 1, :] = rk.reshape(1, tm).astype(jnp.int32)
        gate_ref[r:r + 1, :] = gt.reshape(1, tm)


def _route(xa, na_rows, xb, g, modt, rows_per_mod, wt, bias, tm):
    d = xa.shape[1]
    n_exp = wt.shape[0]
    na = na_rows // tm
    nb = 0 if xb is None else xb.shape[0] // tm
    n = (na + nb) * tm
    tpm = rows_per_mod // tm
    n_mod = modt.shape[0]
    in_specs = [pl.BlockSpec((tm, d), lambda i: (jnp.minimum(i, na - 1), 0))]
    args = [xa]
    if xb is not None:
        in_specs.append(pl.BlockSpec((tm, d), lambda i: (jnp.maximum(i - na, 0), 0)))
        args.append(xb)
    in_specs += [pl.BlockSpec((1, d), lambda i: (0, 0)),
                 pl.BlockSpec((None, _N_MOD, d), lambda i: (jnp.minimum(i // tpm, n_mod - 1), 0, 0)),
                 pl.BlockSpec((n_exp, d), lambda i: (0, 0)),
                 pl.BlockSpec((n_exp, 1), lambda i: (0, 0))]
    args += [g, modt, wt, bias]
    return pl.pallas_call(
        functools.partial(_route_kernel, na=na, two_inputs=xb is not None, n_exp=n_exp),
        grid=(na + nb,),
        in_specs=in_specs,
        out_specs=[pl.BlockSpec((tm, d // 2), lambda i: (i, 0)),
                   pl.BlockSpec((_TOP_K, tm), lambda i: (0, i)),
                   pl.BlockSpec((_TOP_K, tm), lambda i: (0, i)),
                   pl.BlockSpec((_TOP_K, tm), lambda i: (0, i)),
                   pl.BlockSpec((n_exp, 1), lambda i: (0, 0))],
        out_shape=[jax.ShapeDtypeStruct((n, d // 2), _U32),
                   jax.ShapeDtypeStruct((_TOP_K, n), jnp.int32),
                   jax.ShapeDtypeStruct((_TOP_K, n), jnp.int32),
                   jax.ShapeDtypeStruct((_TOP_K, n), _F32),
                   jax.ShapeDtypeStruct((n_exp, 1), _F32)],
        scratch_shapes=[pltpu.VMEM((n_exp, 1), _F32)],
        compiler_params=_params(("arbitrary",)),
    )(*args)


def _dispatch_kernel(dest_ref, hp_ref, xs_ref, sem):
    tm = hp_ref.shape[0]

    def row_copy(n, k):
        return pltpu.make_async_copy(hp_ref.at[pl.ds(n, 1), :], xs_ref.at[pl.ds(dest_ref[k, n], 1), :], sem)

    def body(n, carry):
        for k in range(_TOP_K):
            row_copy(n, k).start(priority=k % 2)
        return carry

    lax.fori_loop(0, tm, body, 0, unroll=4)
    for _ in range(_TOP_K):
        pltpu.make_async_copy(hp_ref, xs_ref.at[pl.ds(0, tm), :], sem).wait()


def _dispatch(hp, dest8, tm):
    n, dh = hp.shape
    return pl.pallas_call(
        _dispatch_kernel,
        grid=(n // tm,),
        in_specs=[pl.BlockSpec((_TOP_K, tm), lambda i: (0, i), memory_space=pltpu.SMEM),
                  pl.BlockSpec((tm, dh), lambda i: (i, 0))],
        out_specs=pl.BlockSpec(memory_space=pl.ANY),
        out_shape=jax.ShapeDtypeStruct((n * _TOP_K, dh), _U32),
        scratch_shapes=[pltpu.SemaphoreType.DMA(())],
        compiler_params=_params(("arbitrary",), disable_bounds_checks=True),
    )(dest8, hp)


def _swiglu_packed(xu, w1, w3, w2):
    dh = xu.shape[1]
    x_lo = _unpack_lo(xu).astype(_BF16)
    x_hi = _unpack_hi(xu).astype(_BF16)
    h1 = (jnp.dot(x_lo, w1[:dh], preferred_element_type=_F32) + jnp.dot(x_hi, w1[dh:], preferred_element_type=_F32))
    h3 = (jnp.dot(x_lo, w3[:dh], preferred_element_type=_F32) + jnp.dot(x_hi, w3[dh:], preferred_element_type=_F32))
    a = (h1 * jax.nn.sigmoid(h1) * h3).astype(_BF16)
    return jnp.dot(a, w2, preferred_element_type=_F32)


def _expert_kernel(tile_ref, exp_ref, lo_ref, hi_ref, first_ref, newexp_ref, valid_ref,
                   x_ref, w1_ref, w3_ref, w2_ref, y_ref, w1b, w3b, w2b):
    v = pl.program_id(0)

    @pl.when(newexp_ref[v] == 1)
    def _():
        w1b[...] = w1_ref[...].astype(_BF16)
        w3b[...] = w3_ref[...].astype(_BF16)
        w2b[...] = w2_ref[...].astype(_BF16)

    @pl.when(valid_ref[v] == 1)
    def _():
        tmx, dh = x_ref.shape
        y = _swiglu_packed(x_ref[...], w1b[...], w3b[...], w2b[...])
        packed = _pack_bf16_pair(y[:, :dh], y[:, dh:])
        rows = lax.broadcasted_iota(jnp.int32, (tmx, 1), 0)
        mine = (rows >= lo_ref[v]) & (rows < hi_ref[v])

        @pl.when(first_ref[v] == 1)
        def _():
            y_ref[...] = jnp.where(mine, packed, jnp.zeros_like(packed))

        @pl.when(first_ref[v] == 0)
        def _():
            y_ref[...] = jnp.where(mine, packed, y_ref[...])


def _cumsum_small(v):
    n = v.shape[0]
    tri = jnp.arange(n)[:, None] >= jnp.arange(n)[None, :]
    return jnp.sum(jnp.where(tri, v[None, :], 0), axis=1).astype(jnp.int32)


def _take_small(table, idx):
    hot = idx[..., None] == jnp.arange(table.shape[0], dtype=jnp.int32)
    return jnp.sum(jnp.where(hot, table, 0), axis=-1).astype(jnp.int32)


def _expert_visits(counts, n_rows, tmx):
    n_exp = counts.shape[0]
    n_tiles = n_rows // tmx
    n_vis = n_tiles + n_exp - 1
    ends = _cumsum_small(counts)
    starts = ends - counts
    t_first = starts // tmx
    t_last = jnp.where(counts > 0, (ends - 1) // tmx, t_first - 1)
    nv = t_last - t_first + 1
    v_end = _cumsum_small(nv)
    v_start = v_end - nv
    total = v_end[-1]
    v = jnp.arange(n_vis, dtype=jnp.int32)
    vc = jnp.minimum(v, total - 1)
    e = jnp.sum((v_end[None, :] <= vc[:, None]).astype(jnp.int32), axis=1)
    tile = _take_small(t_first, e) + (vc - _take_small(v_start, e))
    lo = jnp.clip(_take_small(starts, e) - tile * tmx, 0, tmx)
    hi = jnp.clip(_take_small(ends, e) - tile * tmx, 0, tmx)
    valid = (v < total).astype(jnp.int32)
    prev_tile = jnp.concatenate([jnp.full((1,), -1, jnp.int32), tile[:-1]])
    prev_e = jnp.concatenate([jnp.full((1,), -1, jnp.int32), e[:-1]])
    first = ((tile != prev_tile) & (valid == 1)).astype(jnp.int32)
    newexp = ((e != prev_e) & (valid == 1)).astype(jnp.int32)
    return tile, e, lo, hi, first, newexp, valid


def _experts(xs, counts, w1, w3, w2, layer, tmx):
    p, dh = xs.shape
    _, n_exp, d, f = w1.shape
    meta = _expert_visits(counts, p, tmx)
    n_vis = meta[0].shape[0]
    grid_spec = pltpu.PrefetchScalarGridSpec(
        num_scalar_prefetch=7,
        grid=(n_vis,),
        in_specs=[pl.BlockSpec((tmx, dh), lambda v, t, e, *_: (t[v], 0)),
                  pl.BlockSpec((None, None, d, f), lambda v, t, e, *_: (layer, e[v], 0, 0)),
                  pl.BlockSpec((None, None, d, f), lambda v, t, e, *_: (layer, e[v], 0, 0)),
                  pl.BlockSpec((None, None, f, d), lambda v, t, e, *_: (layer, e[v], 0, 0))],
        out_specs=pl.BlockSpec((tmx, dh), lambda v, t, e, *_: (t[v], 0)),
        scratch_shapes=[pltpu.VMEM((d, f), _BF16), pltpu.VMEM((d, f), _BF16), pltpu.VMEM((f, d), _BF16)],
    )
    return pl.pallas_call(
        _expert_kernel,
        grid_spec=grid_spec,
        out_shape=jax.ShapeDtypeStruct((p, dh), _U32),
        compiler_params=_params(("arbitrary",)),
    )(*meta, xs, w1, w3, w2)


def _combine_kernel(*refs, na, n_steps, two_inputs, final_norm):
    dcur_ref, dnext_ref, w_ref, hp_ref, xa_ref = refs[:5]
    refs = refs[5:]
    xb_ref = None
    if two_inputs:
        xb_ref, refs = refs[0], refs[1:]
    mod_ref, s1_ref, s3_ref, s2_ref = refs[:4]
    refs = refs[4:]
    fg_ref = None
    if final_norm:
        fg_ref, refs = refs[0], refs[1:]
    ys_ref, o_ref, gbuf, sems, ysh_sc = refs
    i = pl.program_id(0)
    slot = i % 2
    tm, dh = hp_ref.shape

    def row_copy(dref, sl, n, k):
        return pltpu.make_async_copy(ys_ref.at[pl.ds(dref[k, n], 1), :], gbuf.at[sl, k, pl.ds(n, 1), :], sems.at[sl])

    def issue(dref, sl):
        def body(n, carry):
            for k in range(_TOP_K):
                row_copy(dref, sl, n, k).start(priority=k % 2)
            return carry

        lax.fori_loop(0, tm, body, 0)

    def wait_slot(sl):
        for k in range(_TOP_K):
            pltpu.make_async_copy(ys_ref.at[pl.ds(0, tm), :], gbuf.at[sl, k], sems.at[sl]).wait()

    @pl.when(i == 0)
    def _():
        issue(dcur_ref, 0)

    ysh_sc[...] = _swiglu_packed(hp_ref[...], s1_ref[...], s3_ref[...], s2_ref[...])
    wait_slot(slot)

    def rows_body(r, carry):
        for t in range(8):
            for k in range(_TOP_K):
                row_copy(dnext_ref, 1 - slot, r * 8 + t, k).start(priority=k % 2)
        rows = pl.ds(pl.multiple_of(r * 8, 8), 8)
        acc_lo = ysh_sc[rows, :dh]
        acc_hi = ysh_sc[rows, dh:]
        for k in range(_TOP_K):
            u = gbuf[slot, k, rows, :]
            wk = w_ref[rows, k:k + 1]
            acc_lo = acc_lo + wk * _unpack_lo(u)
            acc_hi = acc_hi + wk * _unpack_hi(u)
        x = xa_ref[rows, :]
        if two_inputs:
            x = jnp.where(i < na, x, xb_ref[rows, :])
        out_lo = x[:, :dh] + mod_ref[5:6, :dh] * acc_lo
        out_hi = x[:, dh:] + mod_ref[5:6, dh:] * acc_hi
        if final_norm:
            ms = (jnp.sum(out_lo * out_lo, axis=-1, keepdims=True)
                  + jnp.sum(out_hi * out_hi, axis=-1, keepdims=True)) / (2 * dh)
            rs = lax.rsqrt(ms + _NORM_EPS)
            out_lo = out_lo * rs * fg_ref[:, :dh]
            out_hi = out_hi * rs * fg_ref[:, dh:]
        o_ref[rows, :dh] = out_lo
        o_ref[rows, dh:] = out_hi
        return carry

    lax.fori_loop(0, tm // 8, rows_body, 0)

    @pl.when(i == n_steps - 1)
    def _():
        wait_slot(1 - slot)


def _combine(ys, dest8, w8t, hp, xa, na_rows, xb, modt, rows_per_mod, s1, s3, s2, final_g, tm):
    n, dh = hp.shape
    d = 2 * dh
    f = s1.shape[1]
    n_steps = n // tm
    na = na_rows // tm
    tpm = rows_per_mod // tm
    n_mod = modt.shape[0]
    smem_spec = lambda fn: pl.BlockSpec((_TOP_K, tm), fn, memory_space=pltpu.SMEM)
    in_specs = [smem_spec(lambda i: (0, i)),
                smem_spec(lambda i: (0, jnp.minimum(i + 1, n_steps - 1))),
                pl.BlockSpec((tm, _TOP_K), lambda i: (i, 0)),
                pl.BlockSpec((tm, dh), lambda i: (i, 0)),
                pl.BlockSpec((tm, d), lambda i: (jnp.minimum(i, na - 1), 0))]
    args = [dest8, dest8, w8t, hp, xa]
    if xb is not None:
        in_specs.append(pl.BlockSpec((tm, d), lambda i: (jnp.maximum(i - na, 0), 0)))
        args.append(xb)
    in_specs += [pl.BlockSpec((None, _N_MOD, d), lambda i: (jnp.minimum(i // tpm, n_mod - 1), 0, 0)),
                 pl.BlockSpec((d, f), lambda i: (0, 0)),
                 pl.BlockSpec((d, f), lambda i: (0, 0)),
                 pl.BlockSpec((f, d), lambda i: (0, 0))]
    args += [modt, s1, s3, s2]
    if final_g is not None:
        in_specs.append(pl.BlockSpec((1, d), lambda i: (0, 0)))
        args.append(final_g)
    in_specs.append(pl.BlockSpec(memory_space=pl.ANY))
    args.append(ys)
    return pl.pallas_call(
        functools.partial(_combine_kernel, na=na, n_steps=n_steps, two_inputs=xb is not None,
                          final_norm=final_g is not None),
        grid=(n_steps,),
        in_specs=in_specs,
        out_specs=pl.BlockSpec((tm, d), lambda i: (i, 0)),
        out_shape=jax.ShapeDtypeStruct((n, d), _F32),
        scratch_shapes=[pltpu.VMEM((2, _TOP_K, tm, dh), _U32), pltpu.SemaphoreType.DMA((2,)),
                        pltpu.VMEM((tm, d), _F32)],
        compiler_params=_params(("arbitrary",), disable_bounds_checks=True),
    )(*args)


def _moe(xa, na_rows, xb, g, modt, rows_per_mod, router_w, router_bias, w1, w3, w2, layer, s1, s3, s2, final_g):
    hp, eidx8, rank8, gate8, counts = _route(xa, na_rows, xb, g, modt, rows_per_mod,
                                             router_w.T, router_bias.reshape(-1, 1), tm=256)
    counts = counts[:, 0].astype(jnp.int32)
    starts = _cumsum_small(counts) - counts
    dest8 = _take_small(starts, eidx8) + rank8
    xs = _dispatch(hp, dest8, tm=256)
    ys = _experts(xs, counts, w1, w3, w2, layer, tmx=256)
    return _combine(ys, dest8, gate8.T, hp, xa, na_rows, xb, modt, rows_per_mod,
                    s1.astype(_BF16), s3.astype(_BF16), s2.astype(_BF16), final_g, tm=128)


def kernel(x, c, ctx, c_ctx, ada_w, ada_b, norm_g, fourier_wo, fourier_bo, da_wqkv, da_wo, da_lambda, da_subln_g, router_w, router_bias, exp_w1, exp_w3, exp_w2, shared_w1, shared_w3, shared_w2, final_g):
    b, s, d = x.shape
    n_ctx = ctx.shape[1]
    depth = ada_w.shape[0]
    assert depth == 2 and b + 1 <= 8 and s % _GRID_W == 0
    assert s % 2048 == 0 and d % 1024 == 0 and n_ctx % 128 == 0 and d == _V_DIM * (d // _V_DIM)
    nl, nc = b * s, b * n_ctx

    cond8 = jnp.zeros((8, d), _F32).at[:b].set(c).at[b].set(c_ctx)
    mods = _adaln(cond8, ada_w, ada_b).reshape(depth, 8, _N_MOD, d)

    modt = mods[0, :b + 1]
    g_mix, g_moe = norm_g[0, 0:1], norm_g[0, 1:2]
    n2 = 64
    n1 = s // n2
    chan, m1, m2 = _dft_tables(s, n1, n2, d // _FOURIER_GROUPS)
    wo = fourier_wo[0].astype(_BF16)
    bo = fourier_bo[0:1]

    yl = _normmod_chandft(x, g_mix, modt, True, chan, tm=256)
    fl = _seq_dft(yl, m1, m2, n1, n2, tn=256)
    xl = _mm(fl.reshape(nl, d), wo, n_out=d, bias=bo, res=x.reshape(nl, d), gate=modt[:, 2:3], rows_per_gate=s,
             tm=1024, tn=512)

    yc = _normmod_chandft(ctx, g_mix, modt[b:b + 1], False, chan, tm=n_ctx)
    fc = _ctx_seq_dft(yc, tn=512)
    xc = _mm(fc.reshape(nc, d), wo, n_out=d, bias=bo, res=ctx.reshape(nc, d), gate=modt[b:b + 1, 2:3],
             rows_per_gate=nc, tm=n_ctx, tn=512)

    x_all = _moe(xl, nl, xc, g_moe, modt, s, router_w[0], router_bias[0], exp_w1, exp_w3, exp_w2, 0,
                 shared_w1[0], shared_w3[0], shared_w2[0], None)

    modt = mods[1, :b + 1]
    g_mix, g_moe = norm_g[1, 0:1], norm_g[1, 1:2]
    lam_init = 0.8 - 0.6 * float(np.exp(-0.3 * 1))
    wqkv = _rope_friendly_qkv_weights(da_wqkv[0], tn=512)
    cos, sin = _rope_tables(s)

    hl = _normmod(x_all, 0, nl, g_mix, modt, s, 0, tm=512)
    hc = _normmod(x_all, nl, nc, g_mix, modt, nc, b, tm=n_ctx)
    qk, vt = _qkv(hl, wqkv, cos, sin, s, tm=1024, tn=512)
    kvc = _mm(hc, wqkv, n_out=2 * d, w_col0=d, out_dtype=_BF16, tm=n_ctx, tn=512)
    o = _attention(qk.reshape(b, s, 2 * d), vt, kvc.reshape(b, n_ctx, 2 * d), kvc[:, d:].T, da_lambda[0],
                   da_subln_g[0], lam_init, tq=2048, tk=1024)
    xl = _mm(o.reshape(nl, d), da_wo[0].astype(_BF16), n_out=d, res=x_all, gate=modt[:, 2:3], rows_per_gate=s,
             tm=1024, tn=512)

    out = _moe(xl, nl, None, g_moe, modt, s, router_w[1], router_bias[1], exp_w1, exp_w3, exp_w2, 1,
               shared_w1[1], shared_w3[1], shared_w2[1], final_g.reshape(1, d))
    return out.reshape(b, s, d)
```
